```python
import math
import jax, jax.numpy as jnp
from jax import lax
import numpy as np

D_MODEL = 1024
BATCH = 4
SEQ = 8192
DEPTH = 4

N_MIXERS = 4
GROUP_W = D_MODEL // N_MIXERS
CONV_K = 31
S5_CH = 16
S5_GROUPS = GROUP_W // S5_CH
S5_STATE = 64
SHORT_K = 3
DA_HEADS = 4
DA_VDIM = GROUP_W // DA_HEADS
DA_QKDIM = DA_VDIM // 2
ROT_DIM = DA_QKDIM // 4
ROPE_THETA = 500000.0
Q_BLOCK = 128
FFN_HIDDEN = -(-8 * D_MODEL // (3 * 256)) * 256
IN_CONF = 2 * GROUP_W
IN_S5 = GROUP_W
IN_SC = 3 * GROUP_W
IN_DA = 3 * GROUP_W
OFF_S5 = IN_CONF
OFF_SC = OFF_S5 + IN_S5
OFF_DA = OFF_SC + IN_SC
IN_TOTAL = OFF_DA + IN_DA
DEEPNORM_ALPHA = (2.0 * DEPTH) ** 0.25
DEEPNORM_BETA = (8.0 * DEPTH) ** -0.25
LN_EPS = 1e-5

kernel_name = "hybrid_parallel_mixer_encoder"


def layer_norm(x, g, b):
    xf = x.astype(jnp.float32)
    mu = jnp.mean(xf, axis=-1, keepdims=True)
    xc = xf - mu
    var = jnp.mean(xc * xc, axis=-1, keepdims=True)
    return (xc * lax.rsqrt(var + LN_EPS) * g.astype(jnp.float32) + b.astype(jnp.float32)).astype(x.dtype)


def depthwise_conv(x, w):
    k = w.shape[0]
    pad = k // 2
    return lax.conv_general_dilated(
        x, w[:, None, :].astype(x.dtype), window_strides=(1,), padding=[(pad, pad)],
        dimension_numbers=('NWC', 'WIO', 'NWC'), feature_group_count=x.shape[-1])


def conformer_conv(h, dw_w, dw_b, ln_g, ln_b):
    a, g = jnp.split(h, 2, axis=-1)
    z = a * jax.nn.sigmoid(g)
    z = depthwise_conv(z, dw_w) + dw_b.astype(z.dtype)
    z = layer_norm(z, ln_g, ln_b)
    return jax.nn.silu(z)


def _ssm_combine(e1, e2):
    a1r, a1i, b1r, b1i = e1
    a2r, a2i, b2r, b2i = e2
    return (a2r * a1r - a2i * a1i,
            a2r * a1i + a2i * a1r,
            a2r * b1r - a2i * b1i + b2r,
            a2r * b1i + a2i * b1r + b2i)


def s5_mixer(u, a_re, a_im, log_step, b_re, b_im, c_re, c_im, d_skip, w_glu, b_glu):
    bsz, seq = u.shape[0], u.shape[1]
    uf = u.astype(jnp.float32).reshape(bsz, seq, S5_GROUPS, S5_CH)
    y = d_skip.astype(jnp.float32).reshape(S5_GROUPS, S5_CH) * uf
    for direction in (0, 1):
        lr = a_re[direction].astype(jnp.float32)
        li = a_im[direction].astype(jnp.float32)
        step = jnp.exp(log_step[direction].astype(jnp.float32))[:, None]
        mag = jnp.exp(lr * step)
        abr = mag * jnp.cos(li * step)
        abi = mag * jnp.sin(li * step)
        den = lr * lr + li * li
        pr = abr - 1.0
        fr = (pr * lr + abi * li) / den
        fi = (abi * lr - pr * li) / den
        br = b_re[direction].astype(jnp.float32)
        bi = b_im[direction].astype(jnp.float32)
        bbr = fr[..., None] * br - fi[..., None] * bi
        bbi = fr[..., None] * bi + fi[..., None] * br
        bur = jnp.einsum('bsgp,gnp->bsgn', uf, bbr)
        bui = jnp.einsum('bsgp,gnp->bsgn', uf, bbi)
        ar = jnp.broadcast_to(abr, bur.shape)
        ai = jnp.broadcast_to(abi, bur.shape)
        _, _, xr, xi = lax.associative_scan(_ssm_combine, (ar, ai, bur, bui), axis=1,
                                            reverse=(direction == 1))
        y = y + jnp.einsum('bsgn,gpn->bsgp', xr, c_re[direction].astype(jnp.float32)) \
              - jnp.einsum('bsgn,gpn->bsgp', xi, c_im[direction].astype(jnp.float32))
    y = jax.nn.gelu(y.reshape(bsz, seq, GROUP_W))
    y = y * jax.nn.sigmoid(y @ w_glu.astype(jnp.float32) + b_glu.astype(jnp.float32))
    return y.astype(u.dtype)


def short_gated_conv(h, conv_w):
    bg, cg, v = jnp.split(h, 3, axis=-1)
    return bg * depthwise_conv(cg * v, conv_w)


def partial_rope(t, cos, sin):
    half = ROT_DIM // 2
    c = cos[None, :, None, None, :].astype(t.dtype)
    s = sin[None, :, None, None, :].astype(t.dtype)
    x1 = t[..., :half]
    x2 = t[..., half:ROT_DIM]
    return jnp.concatenate([x1 * c - x2 * s, x2 * c + x1 * s, t[..., ROT_DIM:]], axis=-1)


def diff_attention(q, k, v, lam, subln_g, lam_init):
    bsz, seq = q.shape[0], q.shape[1]
    nblk = seq // Q_BLOCK
    qb = (q * (DA_QKDIM ** -0.5)).reshape(bsz, nblk, Q_BLOCK, DA_HEADS, 2, DA_QKDIM).swapaxes(0, 1)

    def block(q_blk):
        s = jnp.einsum('bqhcd,bkhcd->cbhqk', q_blk, k).astype(jnp.float32)
        p = jax.nn.softmax(s, axis=-1)
        a = p[0] - lam * p[1]
        return jnp.einsum('bhqk,bkhe->bqhe', a.astype(v.dtype), v)

    o = lax.map(block, qb).swapaxes(0, 1).reshape(bsz, seq, DA_HEADS, DA_VDIM)
    of = o.astype(jnp.float32)
    of = of * lax.rsqrt(jnp.mean(of * of, axis=-1, keepdims=True) + LN_EPS)
    of = of * subln_g.astype(jnp.float32) * (1.0 - lam_init)
    return of.reshape(bsz, seq, GROUP_W).astype(q.dtype)


def setup_inputs(seed: int = 0) -> dict:
    key = jax.random.key(seed)
    ks = jax.random.split(key, 32)
    f32 = jnp.float32
    L, G, N, P = DEPTH, S5_GROUPS, S5_STATE, S5_CH

    def nrm(k, shape, scale):
        return jax.random.normal(k, shape, f32) * scale

    x = jax.random.normal(ks[0], (BATCH, SEQ, D_MODEL), f32)
    w_in = nrm(ks[1], (L, D_MODEL, IN_TOTAL), D_MODEL ** -0.5)
    w_out = nrm(ks[2], (L, D_MODEL, D_MODEL), D_MODEL ** -0.5 * DEEPNORM_BETA)
    conf_dw_w = nrm(ks[3], (L, CONV_K, GROUP_W), CONV_K ** -0.5)
    conf_dw_b = nrm(ks[4], (L, GROUP_W), 0.02)
    conf_ln_g = 1.0 + nrm(ks[5], (L, GROUP_W), 0.02)
    conf_ln_b = nrm(ks[6], (L, GROUP_W), 0.02)
    s5_a_re = -0.5 + nrm(ks[7], (L, 2, G, N), 0.01)
    s5_a_im = math.pi * jnp.arange(N, dtype=f32) + nrm(ks[8], (L, 2, G, N), 0.01)
    s5_log_step = jax.random.uniform(ks[9], (L, 2, G), f32, math.log(1e-3), math.log(1e-1))
    s5_b_re = nrm(ks[10], (L, 2, G, N, P), (2.0 * P) ** -0.5)
    s5_b_im = nrm(ks[11], (L, 2, G, N, P), (2.0 * P) ** -0.5)
    s5_c_re = nrm(ks[12], (L, 2, G, P, N), (2.0 * N) ** -0.5)
    s5_c_im = nrm(ks[13], (L, 2, G, P, N), (2.0 * N) ** -0.5)
    s5_d = nrm(ks[14], (L, GROUP_W), 1.0)
    s5_w_glu = nrm(ks[15], (L, GROUP_W, GROUP_W), GROUP_W ** -0.5)
    s5_b_glu = nrm(ks[16], (L, GROUP_W), 0.02)
    sc_conv_w = nrm(ks[17], (L, SHORT_K, GROUP_W), SHORT_K ** -0.5)
    da_lq1 = nrm(ks[18], (L, DA_QKDIM), 0.1)
    da_lk1 = nrm(ks[19], (L, DA_QKDIM), 0.1)
    da_lq2 = nrm(ks[20], (L, DA_QKDIM), 0.1)
    da_lk2 = nrm(ks[21], (L, DA_QKDIM), 0.1)
    da_subln_g = 1.0 + nrm(ks[22], (L, DA_VDIM), 0.02)
    ln1_g = 1.0 + nrm(ks[23], (L, D_MODEL), 0.02)
    ln1_b = nrm(ks[24], (L, D_MODEL), 0.02)
    w_ffn1 = nrm(ks[25], (L, D_MODEL, FFN_HIDDEN), D_MODEL ** -0.5)
    w_ffn3 = nrm(ks[26], (L, D_MODEL, FFN_HIDDEN), D_MODEL ** -0.5)
    w_ffn2 = nrm(ks[27], (L, FFN_HIDDEN, D_MODEL), FFN_HIDDEN ** -0.5 * DEEPNORM_BETA)
    ln2_g = 1.0 + nrm(ks[28], (L, D_MODEL), 0.02)
    ln2_b = nrm(ks[29], (L, D_MODEL), 0.02)
    return {"x": x, "w_in": w_in, "w_out": w_out,
            "conf_dw_w": conf_dw_w, "conf_dw_b": conf_dw_b, "conf_ln_g": conf_ln_g, "conf_ln_b": conf_ln_b,
            "s5_a_re": s5_a_re, "s5_a_im": s5_a_im, "s5_log_step": s5_log_step,
            "s5_b_re": s5_b_re, "s5_b_im": s5_b_im, "s5_c_re": s5_c_re, "s5_c_im": s5_c_im,
            "s5_d": s5_d, "s5_w_glu": s5_w_glu, "s5_b_glu": s5_b_glu,
            "sc_conv_w": sc_conv_w,
            "da_lq1": da_lq1, "da_lk1": da_lk1, "da_lq2": da_lq2, "da_lk2": da_lk2, "da_subln_g": da_subln_g,
            "ln1_g": ln1_g, "ln1_b": ln1_b,
            "w_ffn1": w_ffn1, "w_ffn3": w_ffn3, "w_ffn2": w_ffn2,
            "ln2_g": ln2_g, "ln2_b": ln2_b}


def reference(x, w_in, w_out, conf_dw_w, conf_dw_b, conf_ln_g, conf_ln_b,
              s5_a_re, s5_a_im, s5_log_step, s5_b_re, s5_b_im, s5_c_re, s5_c_im,
              s5_d, s5_w_glu, s5_b_glu, sc_conv_w,
              da_lq1, da_lk1, da_lq2, da_lk2, da_subln_g,
              ln1_g, ln1_b, w_ffn1, w_ffn3, w_ffn2, ln2_g, ln2_b):
    bsz, seq = x.shape[0], x.shape[1]
    pos = jnp.arange(seq, dtype=jnp.float32)
    inv_freq = ROPE_THETA ** (-jnp.arange(0, ROT_DIM, 2, dtype=jnp.float32) / ROT_DIM)
    ang = pos[:, None] * inv_freq[None, :]
    cos, sin = jnp.cos(ang), jnp.sin(ang)

    for l in range(DEPTH):
        h = jnp.einsum('bsd,de->bse', x, w_in[l])
        y_a = conformer_conv(h[..., :OFF_S5], conf_dw_w[l], conf_dw_b[l], conf_ln_g[l], conf_ln_b[l])
        y_b = s5_mixer(h[..., OFF_S5:OFF_SC], s5_a_re[l], s5_a_im[l], s5_log_step[l],
                       s5_b_re[l], s5_b_im[l], s5_c_re[l], s5_c_im[l],
                       s5_d[l], s5_w_glu[l], s5_b_glu[l])
        y_c = short_gated_conv(h[..., OFF_SC:OFF_DA], sc_conv_w[l])
        h_da = h[..., OFF_DA:]
        q = h_da[..., :GROUP_W].reshape(bsz, seq, DA_HEADS, 2, DA_QKDIM)
        k = h_da[..., GROUP_W:2 * GROUP_W].reshape(bsz, seq, DA_HEADS, 2, DA_QKDIM)
        v = h_da[..., 2 * GROUP_W:].reshape(bsz, seq, DA_HEADS, DA_VDIM)
        q = partial_rope(q, cos, sin)
        k = partial_rope(k, cos, sin)
        lam_init = 0.8 - 0.6 * math.exp(-0.3 * l)
        lam = (jnp.exp(jnp.sum(da_lq1[l].astype(jnp.float32) * da_lk1[l].astype(jnp.float32)))
               - jnp.exp(jnp.sum(da_lq2[l].astype(jnp.float32) * da_lk2[l].astype(jnp.float32)))
               + lam_init)
        y_d = diff_attention(q, k, v, lam, da_subln_g[l], lam_init)
        mix = jnp.einsum('bse,ed->bsd', jnp.concatenate([y_a, y_b, y_c, y_d], axis=-1), w_out[l])
        x = layer_norm(DEEPNORM_ALPHA * x + mix, ln1_g[l], ln1_b[l])
        ff = jax.nn.silu(x @ w_ffn1[l]) * (x @ w_ffn3[l])
        x = layer_norm(DEEPNORM_ALPHA * x + ff @ w_ffn2[l], ln2_g[l], ln2_b[l])
    return x
```

```python
import functools
import math

import jax
import jax.numpy as jnp
from jax import lax
from jax.experimental import pallas as pl
from jax.experimental.pallas import tpu as pltpu

F32 = jnp.float32
BF16 = jnp.bfloat16

N_MIXERS = 4
CONV_K = 31
S5_CH = 16
S5_STATE = 64
SHORT_K = 3
DA_HEADS = 4
ROPE_THETA = 500000.0
LN_EPS = 1e-5

SUBLANES = 8
LANES = 128
VMEM_LIMIT_BYTES = 56 * 1024 * 1024

PROJ_ROWS = 512
CONV_ROWS = 512
CONV_HALO = 16
CONV_CHUNK = 64
S5_ROWS = 256
ATT_Q = 256
ATT_K = 1024
FFN_ROWS = 512


def _cparams(sem):
    return pltpu.CompilerParams(dimension_semantics=sem, vmem_limit_bytes=VMEM_LIMIT_BYTES)


def _const_spec(shape):
    nd = len(shape)
    return pl.BlockSpec(shape, lambda *_: (0,) * nd)


def _sigmoid(x):
    return 1.0 / (1.0 + jnp.exp(-x))


def _layer_norm(x, g, b):
    mu = jnp.mean(x, axis=-1, keepdims=True)
    xc = x - mu
    var = jnp.mean(xc * xc, axis=-1, keepdims=True)
    return xc * lax.rsqrt(var + LN_EPS) * g + b


def _proj_kernel(gw, x_ref, w_ref, rc_ref, ra_ref, rb_ref,
                 za_ref, u_ref, bg_ref, cv_ref, q_ref, k_ref, vt_ref):
    xb = x_ref[...].astype(BF16)

    def seg(i):
        return jnp.dot(xb, w_ref[:, i * gw:(i + 1) * gw], preferred_element_type=F32)

    za_ref[...] = seg(0) * _sigmoid(seg(1))
    u_ref[...] = seg(2)
    bg_ref[...] = seg(3)
    cv_ref[...] = seg(4) * seg(5)

    rc, ra, rb = rc_ref[...], ra_ref[...], rb_ref[...]
    hd = gw // DA_HEADS
    half = hd // 2 // 4 // 2

    def rope(t):
        return t * rc + pltpu.roll(t, gw - half, axis=1) * ra + pltpu.roll(t, half, axis=1) * rb

    scale = (hd // 2) ** -0.5
    q = rope(seg(6)) * scale
    k = rope(seg(7))
    vt = seg(8).T
    for h in range(DA_HEADS):
        q_ref[0, h] = q[:, h * hd:(h + 1) * hd].astype(BF16)
        k_ref[0, h] = k[:, h * hd:(h + 1) * hd].astype(BF16)
        vt_ref[0, h] = vt[h * hd:(h + 1) * hd, :].astype(BF16)


def _proj_call(x2, w_in, rc, ra, rb, bsz, seq):
    t, d = x2.shape
    gw = d // N_MIXERS
    tm = min(PROJ_ROWS, seq)
    ns = seq // tm
    hd = gw // DA_HEADS
    tok = lambda: pl.BlockSpec((tm, gw), lambda i: (i, 0))
    rope_spec = lambda: pl.BlockSpec((tm, gw), lambda i: (i % ns, 0))
    out_shape = (
        [jax.ShapeDtypeStruct((t, gw), F32)] * 4
        + [jax.ShapeDtypeStruct((bsz, DA_HEADS, seq, hd), BF16)] * 2
        + [jax.ShapeDtypeStruct((bsz, DA_HEADS, hd, seq), BF16)]
    )
    return pl.pallas_call(
        functools.partial(_proj_kernel, gw),
        grid=(t // tm,),
        in_specs=[pl.BlockSpec((tm, d), lambda i: (i, 0)), _const_spec(w_in.shape),
                  rope_spec(), rope_spec(), rope_spec()],
        out_specs=[tok(), tok(), tok(), tok(),
                   pl.BlockSpec((1, DA_HEADS, tm, hd), lambda i: (i // ns, 0, i % ns, 0)),
                   pl.BlockSpec((1, DA_HEADS, tm, hd), lambda i: (i // ns, 0, i % ns, 0)),
                   pl.BlockSpec((1, DA_HEADS, hd, tm), lambda i: (i // ns, 0, 0, i % ns))],
        out_shape=out_shape,
        compiler_params=_cparams(("parallel",)),
        name="proj_in",
    )(x2, w_in, rc, ra, rb)


def _conv_kernel(ts, za_ref, zp_ref, zn_ref, cv_ref, cp_ref, cn_ref, bg_ref,
                 dw_ref, db_ref, g_ref, b_ref, sw_ref, ya_ref, yc_ref, zbuf, cbuf):
    j = pl.program_id(1)
    nj = pl.num_programs(1)
    keep_prev = jnp.where(j > 0, 1.0, 0.0).astype(F32)
    keep_next = jnp.where(j < nj - 1, 1.0, 0.0).astype(F32)
    h = CONV_HALO
    zbuf[0:h, :] = zp_ref[0] * keep_prev
    zbuf[h:h + ts, :] = za_ref[0]
    zbuf[h + ts:2 * h + ts, :] = zn_ref[0] * keep_next
    cbuf[0:h, :] = cp_ref[0] * keep_prev
    cbuf[h:h + ts, :] = cv_ref[0]
    cbuf[h + ts:2 * h + ts, :] = cn_ref[0] * keep_next

    rows = CONV_CHUNK

    win_rows = rows + 2 * h

    def taps(buf, base, w_ref, ntaps):
        win = buf[pl.ds(base, win_rows), :]
        first = h - ntaps // 2
        acc = jnp.zeros((rows, win.shape[-1]), F32)
        for phase in range(SUBLANES):
            offs = [o for o in range(first, first + ntaps) if o % SUBLANES == phase]
            if not offs:
                continue
            rolled = win if phase == 0 else pltpu.roll(win, win_rows - phase, axis=0)
            for o in offs:
                lo = o - phase
                acc = acc + w_ref[o - first:o - first + 1, :] * rolled[lo:lo + rows, :]
        return acc

    def chunk(c, carry):
        base = pl.multiple_of(c * rows, rows)
        acc = taps(zbuf, base, dw_ref, CONV_K)
        z = _layer_norm(acc + db_ref[...], g_ref[...], b_ref[...])
        ya_ref[0, pl.ds(base, rows), :] = (z * _sigmoid(z)).astype(ya_ref.dtype)
        sc = taps(cbuf, base, sw_ref, SHORT_K)
        yc_ref[0, pl.ds(base, rows), :] = (bg_ref[0, pl.ds(base, rows), :] * sc).astype(yc_ref.dtype)
        return carry

    lax.fori_loop(0, ts // rows, chunk, 0)


def _conv_call(za, cv, bg, dw_w, dw_b, ln_g, ln_b, sc_w):
    bsz, seq, gw = za.shape
    ts = min(CONV_ROWS, seq)
    r = ts // CONV_HALO
    nh = seq // CONV_HALO
    main = lambda: pl.BlockSpec((1, ts, gw), lambda b, j: (b, j, 0))
    prev = lambda: pl.BlockSpec((1, CONV_HALO, gw), lambda b, j: (b, jnp.maximum(j * r - 1, 0), 0))
    nxt = lambda: pl.BlockSpec((1, CONV_HALO, gw), lambda b, j: (b, jnp.minimum((j + 1) * r, nh - 1), 0))
    return pl.pallas_call(
        functools.partial(_conv_kernel, ts),
        grid=(bsz, seq // ts),
        in_specs=[main(), prev(), nxt(), main(), prev(), nxt(), main(),
                  _const_spec(dw_w.shape), _const_spec(dw_b.shape), _const_spec(ln_g.shape),
                  _const_spec(ln_b.shape), _const_spec(sc_w.shape)],
        out_specs=[main(), main()],
        out_shape=[jax.ShapeDtypeStruct((bsz, seq, gw), BF16)] * 2,
        scratch_shapes=[pltpu.VMEM((ts + 2 * CONV_HALO, gw), F32)] * 2,
        compiler_params=_cparams(("parallel", "parallel")),
        name="dwconv",
    )(za, za, za, cv, cv, cv, bg, dw_w, dw_b, ln_g, ln_b, sc_w)


def _s5_kernel(ts, u_ref, bb_ref, cc_ref, mre_ref, mim_ref, cre_ref, cim_ref, y_ref,
               pr_ref, pi_ref, xs_ref, car_ref):
    d = pl.program_id(0)
    j = pl.program_id(2)
    ns = cre_ref.shape[-1]

    @pl.when(j == 0)
    def _():
        car_ref[...] = jnp.zeros_like(car_ref)

    bu = jnp.dot(u_ref[0].astype(BF16), bb_ref[0], preferred_element_type=F32)
    pr_ref[...] = bu[:, :ns]
    pi_ref[...] = bu[:, ns:]

    def run(reverse):
        ngroups = ts // SUBLANES

        def group(i, carry):
            cr, ci = carry
            g = (ngroups - 1 - i) if reverse else i
            base = pl.multiple_of(g * SUBLANES, SUBLANES)
            xr = pr_ref[pl.ds(base, SUBLANES), :]
            xi = pi_ref[pl.ds(base, SUBLANES), :]
            for lvl in range(3):
                k = 1 << lvl
                shift = (SUBLANES - k) if reverse else k
                sr = pltpu.roll(xr, shift, axis=0)
                si = pltpu.roll(xi, shift, axis=0)
                ar = mre_ref[0, lvl]
                ai = mim_ref[0, lvl]
                xr, xi = xr + (ar * sr - ai * si), xi + (ar * si + ai * sr)
            ar = cre_ref[0]
            ai = cim_ref[0]
            xr = xr + (ar * cr - ai * ci)
            xi = xi + (ar * ci + ai * cr)
            xs_ref[pl.ds(base, SUBLANES), 0:ns] = xr
            xs_ref[pl.ds(base, SUBLANES), ns:2 * ns] = xi
            edge = 0 if reverse else SUBLANES - 1
            cr = jnp.broadcast_to(xr[edge:edge + 1, :], xr.shape)
            ci = jnp.broadcast_to(xi[edge:edge + 1, :], xi.shape)
            return cr, ci

        cr, ci = lax.fori_loop(0, ngroups, group, (car_ref[0], car_ref[1]))
        car_ref[0] = cr
        car_ref[1] = ci

    @pl.when(d == 0)
    def _():
        run(False)

    @pl.when(d == 1)
    def _():
        run(True)

    y_ref[0, 0] = jnp.dot(xs_ref[...].astype(BF16), cc_ref[0], preferred_element_type=F32)


def _s5_call(u, bb, cc, mre, mim, cre, cim):
    bsz, seq, gw = u.shape
    ns = cre.shape[-1]
    ts = min(S5_ROWS, seq)
    nj = seq // ts
    tidx = lambda d, j: j + d * (nj - 1 - 2 * j)
    dspec = lambda a: pl.BlockSpec((1,) + a.shape[1:], lambda d, b, j: (d,) + (0,) * (a.ndim - 1))
    return pl.pallas_call(
        functools.partial(_s5_kernel, ts),
        grid=(2, bsz, nj),
        in_specs=[pl.BlockSpec((1, ts, gw), lambda d, b, j: (b, tidx(d, j), 0)),
                  dspec(bb), dspec(cc), dspec(mre), dspec(mim), dspec(cre), dspec(cim)],
        out_specs=pl.BlockSpec((1, 1, ts, gw), lambda d, b, j: (d, b, tidx(d, j), 0)),
        out_shape=jax.ShapeDtypeStruct((2, bsz, seq, gw), F32),
        scratch_shapes=[pltpu.VMEM((ts, ns), F32), pltpu.VMEM((ts, ns), F32),
                        pltpu.VMEM((ts, 2 * ns), F32), pltpu.VMEM((2, SUBLANES, ns), F32)],
        compiler_params=_cparams(("arbitrary", "arbitrary", "arbitrary")),
        name="s5_scan",
    )(u, bb, cc, mre, mim, cre, cim)


def _s5_params(a_re, a_im, log_step, b_re, b_im, c_re, c_im):
    g_, n_ = a_re.shape[1], a_re.shape[2]
    p_ = b_re.shape[-1]
    ns = g_ * n_
    step = jnp.exp(log_step.astype(F32))[..., None]
    lr, li = a_re.astype(F32), a_im.astype(F32)
    mag = jnp.exp(lr * step)
    abr, abi = mag * jnp.cos(li * step), mag * jnp.sin(li * step)
    den = lr * lr + li * li
    pr = abr - 1.0
    fr = (pr * lr + abi * li) / den
    fi = (abi * lr - pr * li) / den
    br, bi = b_re.astype(F32), b_im.astype(F32)
    bbr = fr[..., None] * br - fi[..., None] * bi
    bbi = fr[..., None] * bi + fi[..., None] * br
    eye = jnp.eye(g_, dtype=F32)

    def blockdiag_in(w):
        return jnp.einsum('dgnp,gh->dgphn', w, eye).reshape(2, g_ * p_, ns)

    def blockdiag_out(w):
        return jnp.einsum('dgpn,gh->dgnhp', w, eye).reshape(2, ns, g_ * p_)

    bb = jnp.concatenate([blockdiag_in(bbr), blockdiag_in(bbi)], axis=-1).astype(BF16)
    cc = jnp.concatenate([blockdiag_out(c_re.astype(F32)), -blockdiag_out(c_im.astype(F32))],
                         axis=1).astype(BF16)

    def apow(k):
        m = jnp.exp(lr * step * k)
        return (m * jnp.cos(li * step * k)).reshape(2, ns), (m * jnp.sin(li * step * k)).reshape(2, ns)

    rows = jnp.arange(SUBLANES)
    mre, mim = [], []
    for lvl in range(3):
        k = 1 << lvl
        pr_k, pi_k = apow(float(k))
        fwd = (rows >= k).astype(F32)[:, None]
        bwd = (rows <= SUBLANES - 1 - k).astype(F32)[:, None]
        mask = jnp.stack([fwd, bwd])
        mre.append(mask * pr_k[:, None, :])
        mim.append(mask * pi_k[:, None, :])
    mre = jnp.stack(mre, axis=1)
    mim = jnp.stack(mim, axis=1)
    expo = jnp.stack([rows + 1, SUBLANES - rows]).astype(F32)[..., None]
    cm = jnp.exp((lr * step).reshape(2, 1, ns) * expo)
    ang = (li * step).reshape(2, 1, ns) * expo
    cre, cim = cm * jnp.cos(ang), cm * jnp.sin(ang)
    return bb, cc, mre, mim, cre, cim


def _attn_kernel(tk, lam_ref, q_ref, k_ref, vt_ref, g_ref, o_ref, va_ref, s_ref, p_ref):
    hd = q_ref.shape[-1]
    dq = hd // 2
    tq = q_ref.shape[2]
    seq = k_ref.shape[2]
    rows_aug = va_ref.shape[0]

    @pl.when(pl.program_id(2) == 0)
    def _():
        va_ref[0:hd, :] = vt_ref[0, 0]
        row = lax.broadcasted_iota(jnp.int32, (rows_aug - hd, seq), 0)
        va_ref[hd:rows_aug, :] = jnp.where(row == 0, 1.0, 0.0).astype(va_ref.dtype)

    q = q_ref[0, 0]
    lane = lax.broadcasted_iota(jnp.int32, q.shape, 1)
    zero = jnp.zeros_like(q)
    qq = jnp.concatenate([jnp.where(lane < dq, q, zero), jnp.where(lane >= dq, q, zero)], axis=0)

    def block(i, carry):
        m, acc1, acc2 = carry
        k0 = pl.multiple_of(i * tk, tk)
        kb = k_ref[0, 0, pl.ds(k0, tk), :]
        s_ref[...] = lax.dot_general(kb, qq, (((1,), (1,)), ((), ())), preferred_element_type=F32)
        m_new = jnp.maximum(m, jnp.max(s_ref[...], axis=0, keepdims=True))
        alpha = jnp.exp(m - m_new)
        p_ref[...] = jnp.exp(s_ref[...] - m_new).astype(p_ref.dtype)
        vb = va_ref[:, pl.ds(k0, tk)]
        acc1 = acc1 * alpha[:, 0:tq] + jnp.dot(vb, p_ref[:, 0:tq], preferred_element_type=F32)
        acc2 = acc2 * alpha[:, tq:2 * tq] + jnp.dot(vb, p_ref[:, tq:2 * tq], preferred_element_type=F32)
        return m_new, acc1, acc2

    m0 = jnp.full((1, 2 * tq), -jnp.inf, F32)
    a0 = jnp.zeros((rows_aug, tq), F32)
    _, acc1, acc2 = lax.fori_loop(0, seq // tk, block, (m0, a0, a0))
    lam = lam_ref[0]
    post = lam_ref[1]
    o = acc1[0:hd, :] / acc1[hd:hd + 1, :] - lam * (acc2[0:hd, :] / acc2[hd:hd + 1, :])
    o = o * lax.rsqrt(jnp.mean(o * o, axis=0, keepdims=True) + LN_EPS)
    o_ref[0] = (o * g_ref[...] * post).astype(o_ref.dtype)


def _attn_call(lam2, q, k, vt, g_col):
    bsz, nh, seq, hd = q.shape
    tq = min(ATT_Q, seq)
    tk = min(ATT_K, seq)
    rows_aug = hd + 16
    return pl.pallas_call(
        functools.partial(_attn_kernel, tk),
        grid=(bsz, nh, seq // tq),
        in_specs=[pl.BlockSpec(memory_space=pltpu.SMEM),
                  pl.BlockSpec((1, 1, tq, hd), lambda b, h, i: (b, h, i, 0)),
                  pl.BlockSpec((1, 1, seq, hd), lambda b, h, i: (b, h, 0, 0)),
                  pl.BlockSpec((1, 1, hd, seq), lambda b, h, i: (b, h, 0, 0)),
                  _const_spec(g_col.shape)],
        out_specs=pl.BlockSpec((1, hd, tq), lambda b, h, i: (b, h, i)),
        out_shape=jax.ShapeDtypeStruct((bsz, nh * hd, seq), BF16),
        scratch_shapes=[pltpu.VMEM((rows_aug, seq), BF16), pltpu.VMEM((tk, 2 * tq), F32),
                        pltpu.VMEM((tk, 2 * tq), BF16)],
        compiler_params=_cparams(("parallel", "parallel", "arbitrary")),
        name="diff_attn",
    )(lam2, q, k, vt, g_col)


def _gelu_tanh(x):
    return 0.5 * x * (1.0 + jnp.tanh(math.sqrt(2.0 / math.pi) * (x + 0.044715 * (x * x * x))))


def _ffn_kernel(alpha, splits, x_ref, ya_ref, u_ref, yf_ref, yb_ref, yc_ref, ydt_ref,
                wo_ref, sd_ref, wg_ref, bg_ref, g1_ref, b1_ref, w1_ref, w3_ref, w2_ref,
                g2_ref, b2_ref, o_ref):
    gw = ya_ref.shape[-1]
    x = x_ref[...]
    ys = _gelu_tanh(sd_ref[...] * u_ref[...] + yf_ref[0] + yb_ref[0])
    gate = jnp.dot(ys.astype(BF16), wg_ref[...], preferred_element_type=F32) + bg_ref[...]
    y_b = (ys * _sigmoid(gate)).astype(BF16)
    mix = jnp.dot(ya_ref[...], wo_ref[0:gw, :], preferred_element_type=F32)
    mix += jnp.dot(y_b, wo_ref[gw:2 * gw, :], preferred_element_type=F32)
    mix += jnp.dot(yc_ref[...], wo_ref[2 * gw:3 * gw, :], preferred_element_type=F32)
    mix += lax.dot_general(ydt_ref[0], wo_ref[3 * gw:4 * gw, :], (((0,), (0,)), ((), ())),
                           preferred_element_type=F32)
    x1 = _layer_norm(alpha * x + mix, g1_ref[...], b1_ref[...])
    x1b = x1.astype(BF16)
    ff = None
    for lo, hi in splits:
        h1 = jnp.dot(x1b, w1_ref[:, lo:hi], preferred_element_type=F32)
        h3 = jnp.dot(x1b, w3_ref[:, lo:hi], preferred_element_type=F32)
        act = (h1 * _sigmoid(h1) * h3).astype(BF16)
        part = jnp.dot(act, w2_ref[lo:hi, :], preferred_element_type=F32)
        ff = part if ff is None else ff + part
    o_ref[...] = _layer_norm(alpha * x1 + ff, g2_ref[...], b2_ref[...])


def _ffn_call(alpha, x2, ya, u, y2, yc, ydt, wo, sd, wg, bgl, g1, b1, w1, w3, w2, g2, b2, bsz, seq):
    t, d = x2.shape
    gw = d // N_MIXERS
    tm = min(FFN_ROWS, seq)
    ns = seq // tm
    hid = w1.shape[-1]
    cut = (hid // 2 + 255) // 256 * 256
    splits = ((0, cut), (cut, hid)) if 0 < cut < hid else ((0, hid),)
    tok = lambda w: pl.BlockSpec((tm, w), lambda i: (i, 0))
    single = lambda a: pl.BlockSpec(a.shape, lambda i: (0,) * a.ndim, pipeline_mode=pl.Buffered(1))
    return pl.pallas_call(
        functools.partial(_ffn_kernel, alpha, splits),
        grid=(t // tm,),
        in_specs=[tok(d), tok(gw), tok(gw),
                  pl.BlockSpec((1, tm, gw), lambda i: (0, i, 0)),
                  pl.BlockSpec((1, tm, gw), lambda i: (1, i, 0)),
                  tok(gw),
                  pl.BlockSpec((1, gw, tm), lambda i: (i // ns, 0, i % ns)),
                  single(wo), single(sd), single(wg), single(bgl), single(g1), single(b1),
                  single(w1), single(w3), single(w2), single(g2), single(b2)],
        out_specs=tok(d),
        out_shape=jax.ShapeDtypeStruct((t, d), F32),
        compiler_params=_cparams(("parallel",)),
        name="out_ffn",
    )(x2, ya, u, y2, y2, yc, ydt, wo, sd, wg, bgl, g1, b1, w1, w3, w2, g2, b2)


def _rope_tables(seq, gw):
    hd = gw // DA_HEADS
    dq = hd // 2
    rot = dq // 4
    half = rot // 2
    pos = jnp.arange(seq, dtype=F32)
    inv_freq = ROPE_THETA ** (-jnp.arange(0, rot, 2, dtype=F32) / rot)
    ang = pos[:, None] * inv_freq[None, :]
    cos, sin = jnp.cos(ang), jnp.sin(ang)
    lane = jnp.arange(gw) % dq
    idx = lane % half
    lo = (lane < half)[None, :]
    hi = ((lane >= half) & (lane < rot))[None, :]
    cos_l, sin_l = cos[:, idx], sin[:, idx]
    rc = jnp.where(lo | hi, cos_l, 1.0)
    ra = jnp.where(lo, -sin_l, 0.0)
    rb = jnp.where(hi, sin_l, 0.0)
    return rc.astype(F32), ra.astype(F32), rb.astype(F32)


def kernel(x, w_in, w_out, conf_dw_w, conf_dw_b, conf_ln_g, conf_ln_b, s5_a_re, s5_a_im, s5_log_step,
           s5_b_re, s5_b_im, s5_c_re, s5_c_im, s5_d, s5_w_glu, s5_b_glu, sc_conv_w, da_lq1, da_lk1,
           da_lq2, da_lk2, da_subln_g, ln1_g, ln1_b, w_ffn1, w_ffn3, w_ffn2, ln2_g, ln2_b):
    bsz, seq, d = x.shape
    depth = w_in.shape[0]
    gw = d // N_MIXERS
    alpha = (2.0 * depth) ** 0.25
    rc, ra, rb = _rope_tables(seq, gw)
    row = lambda a: a.astype(F32).reshape(1, -1)
    x2 = x.reshape(bsz * seq, d)
    for l in range(depth):
        za, u, bg, cv, q, k, vt = _proj_call(x2, w_in[l].astype(BF16), rc, ra, rb, bsz, seq)
        b3 = lambda a: a.reshape(bsz, seq, gw)
        ya, yc = _conv_call(b3(za), b3(cv), b3(bg), conf_dw_w[l].astype(F32), row(conf_dw_b[l]),
                            row(conf_ln_g[l]), row(conf_ln_b[l]), sc_conv_w[l].astype(F32))
        s5p = _s5_params(s5_a_re[l], s5_a_im[l], s5_log_step[l], s5_b_re[l], s5_b_im[l],
                         s5_c_re[l], s5_c_im[l])
        y2 = _s5_call(b3(u), *s5p)
        lam_init = 0.8 - 0.6 * math.exp(-0.3 * l)
        lam = (jnp.exp(jnp.sum(da_lq1[l].astype(F32) * da_lk1[l].astype(F32)))
               - jnp.exp(jnp.sum(da_lq2[l].astype(F32) * da_lk2[l].astype(F32))) + lam_init)
        lam2 = jnp.stack([lam, jnp.asarray(1.0 - lam_init, F32)]).astype(F32)
        ydt = _attn_call(lam2, q, k, vt, da_subln_g[l].astype(F32).reshape(-1, 1))
        x2 = _ffn_call(alpha, x2, ya.reshape(-1, gw), u, y2.reshape(2, -1, gw), yc.reshape(-1, gw), ydt,
                       w_out[l].astype(BF16), row(s5_d[l]), s5_w_glu[l].astype(BF16), row(s5_b_glu[l]),
                       row(ln1_g[l]), row(ln1_b[l]), w_ffn1[l].astype(BF16), w_ffn3[l].astype(BF16),
                       w_ffn2[l].astype(BF16), row(ln2_g[l]), row(ln2_b[l]), bsz, seq)
    return x2.reshape(bsz, seq, d)
```

```python
import functools
import math

import jax
import jax.numpy as jnp
from jax import lax
from jax.experimental import pallas as pl
from jax.experimental.pallas import tpu as pltpu

F32 = jnp.float32
BF16 = jnp.bfloat16

N_MIXERS = 4
CONV_K = 31
S5_CH = 16
S5_STATE = 64
SHORT_K = 3
DA_HEADS = 4
ROPE_THETA = 500000.0
LN_EPS = 1e-5

SUBLANES = 8
LANES = 128
VMEM_LIMIT_BYTES = 56 * 1024 * 1024

PROJ_ROWS = 512
CONV_ROWS = 512
CONV_HALO = 16
CONV_CHUNK = 64
S5_ROWS = 256
ATT_Q = 256
ATT_K = 1024
FFN_ROWS = 512


def _cparams(sem):
    return pltpu.CompilerParams(dimension_semantics=sem, vmem_limit_bytes=VMEM_LIMIT_BYTES)


def _const_spec(shape):
    nd = len(shape)
    return pl.BlockSpec(shape, lambda *_: (0,) * nd)


def _sigmoid(x):
    return 1.0 / (1.0 + jnp.exp(-x))


def _layer_norm(x, g, b):
    mu = jnp.mean(x, axis=-1, keepdims=True)
    xc = x - mu
    var = jnp.mean(xc * xc, axis=-1, keepdims=True)
    return xc * lax.rsqrt(var + LN_EPS) * g + b


def _proj_kernel(gw, x_ref, w_ref, rc_ref, ra_ref, rb_ref,
                 za_ref, u_ref, bg_ref, cv_ref, q_ref, k_ref, vt_ref):
    xb = x_ref[...].astype(BF16)

    def seg(i):
        return jnp.dot(xb, w_ref[:, i * gw:(i + 1) * gw], preferred_element_type=F32)

    za_ref[...] = seg(0) * _sigmoid(seg(1))
    u_ref[...] = seg(2)
    bg_ref[...] = seg(3)
    cv_ref[...] = seg(4) * seg(5)

    rc, ra, rb = rc_ref[...], ra_ref[...], rb_ref[...]
    hd = gw // DA_HEADS
    half = hd // 2 // 4 // 2

    def rope(t):
        return t * rc + pltpu.roll(t, gw - half, axis=1) * ra + pltpu.roll(t, half, axis=1) * rb

    scale = (hd // 2) ** -0.5 * math.log2(math.e)
    q = rope(seg(6)) * scale
    k = rope(seg(7))
    vt = seg(8).T
    for h in range(DA_HEADS):
        q_ref[0, h] = q[:, h * hd:(h + 1) * hd].astype(BF16)
        k_ref[0, h] = k[:, h * hd:(h + 1) * hd].astype(BF16)
        vt_ref[0, h] = vt[h * hd:(h + 1) * hd, :].astype(BF16)


def _proj_call(x2, w_in, rc, ra, rb, bsz, seq):
    t, d = x2.shape
    gw = d // N_MIXERS
    tm = min(PROJ_ROWS, seq)
    ns = seq // tm
    hd = gw // DA_HEADS
    tok = lambda: pl.BlockSpec((tm, gw), lambda i: (i, 0))
    rope_spec = lambda: pl.BlockSpec((tm, gw), lambda i: (i % ns, 0))
    out_shape = (
        [jax.ShapeDtypeStruct((t, gw), F32)] * 4
        + [jax.ShapeDtypeStruct((bsz, DA_HEADS, seq, hd), BF16)] * 2
        + [jax.ShapeDtypeStruct((bsz, DA_HEADS, hd, seq), BF16)]
    )
    return pl.pallas_call(
        functools.partial(_proj_kernel, gw),
        grid=(t // tm,),
        in_specs=[pl.BlockSpec((tm, d), lambda i: (i, 0)), _const_spec(w_in.shape),
                  rope_spec(), rope_spec(), rope_spec()],
        out_specs=[tok(), tok(), tok(), tok(),
                   pl.BlockSpec((1, DA_HEADS, tm, hd), lambda i: (i // ns, 0, i % ns, 0)),
                   pl.BlockSpec((1, DA_HEADS, tm, hd), lambda i: (i // ns, 0, i % ns, 0)),
                   pl.BlockSpec((1, DA_HEADS, hd, tm), lambda i: (i // ns, 0, 0, i % ns))],
        out_shape=out_shape,
        compiler_params=_cparams(("parallel",)),
        name="proj_in",
    )(x2, w_in, rc, ra, rb)


def _conv_kernel(ts, za_ref, zp_ref, zn_ref, cv_ref, cp_ref, cn_ref, bg_ref,
                 dw_ref, db_ref, g_ref, b_ref, sw_ref, ya_ref, yc_ref, zbuf, cbuf):
    j = pl.program_id(1)
    nj = pl.num_programs(1)
    keep_prev = jnp.where(j > 0, 1.0, 0.0).astype(F32)
    keep_next = jnp.where(j < nj - 1, 1.0, 0.0).astype(F32)
    h = CONV_HALO
    zbuf[0:h, :] = zp_ref[0] * keep_prev
    zbuf[h:h + ts, :] = za_ref[0]
    zbuf[h + ts:2 * h + ts, :] = zn_ref[0] * keep_next
    cbuf[0:h, :] = cp_ref[0] * keep_prev
    cbuf[h:h + ts, :] = cv_ref[0]
    cbuf[h + ts:2 * h + ts, :] = cn_ref[0] * keep_next

    rows = CONV_CHUNK

    win_rows = rows + 2 * h

    def taps(buf, base, w_ref, ntaps):
        win = buf[pl.ds(base, win_rows), :]
        first = h - ntaps // 2
        acc = jnp.zeros((rows, win.shape[-1]), F32)
        for phase in range(SUBLANES):
            offs = [o for o in range(first, first + ntaps) if o % SUBLANES == phase]
            if not offs:
                continue
            rolled = win if phase == 0 else pltpu.roll(win, win_rows - phase, axis=0)
            for o in offs:
                lo = o - phase
                acc = acc + w_ref[o - first:o - first + 1, :] * rolled[lo:lo + rows, :]
        return acc

    def chunk(c, carry):
        base = pl.multiple_of(c * rows, rows)
        acc = taps(zbuf, base, dw_ref, CONV_K)
        z = _layer_norm(acc + db_ref[...], g_ref[...], b_ref[...])
        ya_ref[0, pl.ds(base, rows), :] = (z * _sigmoid(z)).astype(ya_ref.dtype)
        sc = taps(cbuf, base, sw_ref, SHORT_K)
        yc_ref[0, pl.ds(base, rows), :] = (bg_ref[0, pl.ds(base, rows), :] * sc).astype(yc_ref.dtype)
        return carry

    lax.fori_loop(0, ts // rows, chunk, 0)


def _conv_call(za, cv, bg, dw_w, dw_b, ln_g, ln_b, sc_w):
    bsz, seq, gw = za.shape
    ts = min(CONV_ROWS, seq)
    r = ts // CONV_HALO
    nh = seq // CONV_HALO
    main = lambda: pl.BlockSpec((1, ts, gw), lambda b, j: (b, j, 0))
    prev = lambda: pl.BlockSpec((1, CONV_HALO, gw), lambda b, j: (b, jnp.maximum(j * r - 1, 0), 0))
    nxt = lambda: pl.BlockSpec((1, CONV_HALO, gw), lambda b, j: (b, jnp.minimum((j + 1) * r, nh - 1), 0))
    return pl.pallas_call(
        functools.partial(_conv_kernel, ts),
        grid=(bsz, seq // ts),
        in_specs=[main(), prev(), nxt(), main(), prev(), nxt(), main(),
                  _const_spec(dw_w.shape), _const_spec(dw_b.shape), _const_spec(ln_g.shape),
                  _const_spec(ln_b.shape), _const_spec(sc_w.shape)],
        out_specs=[main(), main()],
        out_shape=[jax.ShapeDtypeStruct((bsz, seq, gw), BF16)] * 2,
        scratch_shapes=[pltpu.VMEM((ts + 2 * CONV_HALO, gw), F32)] * 2,
        compiler_params=_cparams(("parallel", "parallel")),
        name="dwconv",
    )(za, za, za, cv, cv, cv, bg, dw_w, dw_b, ln_g, ln_b, sc_w)


def _s5_kernel(ts, u_ref, bb_ref, cc_ref, mre_ref, mim_ref, cre_ref, cim_ref, y_ref,
               pr_ref, pi_ref, xs_ref, car_ref):
    d = pl.program_id(0)
    j = pl.program_id(2)
    ns = cre_ref.shape[-1]

    @pl.when(j == 0)
    def _():
        car_ref[...] = jnp.zeros_like(car_ref)

    bu = jnp.dot(u_ref[0].astype(BF16), bb_ref[0], preferred_element_type=F32)
    pr_ref[...] = bu[:, :ns]
    pi_ref[...] = bu[:, ns:]

    def run(reverse):
        ngroups = ts // SUBLANES

        def group(i, carry):
            cr, ci = carry
            g = (ngroups - 1 - i) if reverse else i
            base = pl.multiple_of(g * SUBLANES, SUBLANES)
            xr = pr_ref[pl.ds(base, SUBLANES), :]
            xi = pi_ref[pl.ds(base, SUBLANES), :]
            for lvl in range(3):
                k = 1 << lvl
                shift = (SUBLANES - k) if reverse else k
                sr = pltpu.roll(xr, shift, axis=0)
                si = pltpu.roll(xi, shift, axis=0)
                ar = mre_ref[0, lvl]
                ai = mim_ref[0, lvl]
                xr, xi = xr + (ar * sr - ai * si), xi + (ar * si + ai * sr)
            ar = cre_ref[0]
            ai = cim_ref[0]
            xr = xr + (ar * cr - ai * ci)
            xi = xi + (ar * ci + ai * cr)
            xs_ref[pl.ds(base, SUBLANES), 0:ns] = xr
            xs_ref[pl.ds(base, SUBLANES), ns:2 * ns] = xi
            edge = 0 if reverse else SUBLANES - 1
            cr = jnp.broadcast_to(xr[edge:edge + 1, :], xr.shape)
            ci = jnp.broadcast_to(xi[edge:edge + 1, :], xi.shape)
            return cr, ci

        cr, ci = lax.fori_loop(0, ngroups, group, (car_ref[0], car_ref[1]))
        car_ref[0] = cr
        car_ref[1] = ci

    @pl.when(d == 0)
    def _():
        run(False)

    @pl.when(d == 1)
    def _():
        run(True)

    y_ref[0, 0] = jnp.dot(xs_ref[...].astype(BF16), cc_ref[0], preferred_element_type=F32)


def _s5_call(u, bb, cc, mre, mim, cre, cim):
    bsz, seq, gw = u.shape
    ns = cre.shape[-1]
    ts = min(S5_ROWS, seq)
    nj = seq // ts
    tidx = lambda d, j: j + d * (nj - 1 - 2 * j)
    dspec = lambda a: pl.BlockSpec((1,) + a.shape[1:], lambda d, b, j: (d,) + (0,) * (a.ndim - 1))
    return pl.pallas_call(
        functools.partial(_s5_kernel, ts),
        grid=(2, bsz, nj),
        in_specs=[pl.BlockSpec((1, ts, gw), lambda d, b, j: (b, tidx(d, j), 0)),
                  dspec(bb), dspec(cc), dspec(mre), dspec(mim), dspec(cre), dspec(cim)],
        out_specs=pl.BlockSpec((1, 1, ts, gw), lambda d, b, j: (d, b, tidx(d, j), 0)),
        out_shape=jax.ShapeDtypeStruct((2, bsz, seq, gw), F32),
        scratch_shapes=[pltpu.VMEM((ts, ns), F32), pltpu.VMEM((ts, ns), F32),
                        pltpu.VMEM((ts, 2 * ns), F32), pltpu.VMEM((2, SUBLANES, ns), F32)],
        compiler_params=_cparams(("arbitrary", "arbitrary", "arbitrary")),
        name="s5_scan",
    )(u, bb, cc, mre, mim, cre, cim)


def _s5_params(a_re, a_im, log_step, b_re, b_im, c_re, c_im):
    g_, n_ = a_re.shape[1], a_re.shape[2]
    p_ = b_re.shape[-1]
    ns = g_ * n_
    step = jnp.exp(log_step.astype(F32))[..., None]
    lr, li = a_re.astype(F32), a_im.astype(F32)
    mag = jnp.exp(lr * step)
    abr, abi = mag * jnp.cos(li * step), mag * jnp.sin(li * step)
    den = lr * lr + li * li
    pr = abr - 1.0
    fr = (pr * lr + abi * li) / den
    fi = (abi * lr - pr * li) / den
    br, bi = b_re.astype(F32), b_im.astype(F32)
    bbr = fr[..., None] * br - fi[..., None] * bi
    bbi = fr[..., None] * bi + fi[..., None] * br
    eye = jnp.eye(g_, dtype=F32)

    def blockdiag_in(w):
        return jnp.einsum('dgnp,gh->dgphn', w, eye).reshape(2, g_ * p_, ns)

    def blockdiag_out(w):
        return jnp.einsum('dgpn,gh->dgnhp', w, eye).reshape(2, ns, g_ * p_)

    bb = jnp.concatenate([blockdiag_in(bbr), blockdiag_in(bbi)], axis=-1).astype(BF16)
    cc = jnp.concatenate([blockdiag_out(c_re.astype(F32)), -blockdiag_out(c_im.astype(F32))],
                         axis=1).astype(BF16)

    def apow(k):
        m = jnp.exp(lr * step * k)
        return (m * jnp.cos(li * step * k)).reshape(2, ns), (m * jnp.sin(li * step * k)).reshape(2, ns)

    rows = jnp.arange(SUBLANES)
    mre, mim = [], []
    for lvl in range(3):
        k = 1 << lvl
        pr_k, pi_k = apow(float(k))
        fwd = (rows >= k).astype(F32)[:, None]
        bwd = (rows <= SUBLANES - 1 - k).astype(F32)[:, None]
        mask = jnp.stack([fwd, bwd])
        mre.append(mask * pr_k[:, None, :])
        mim.append(mask * pi_k[:, None, :])
    mre = jnp.stack(mre, axis=1)
    mim = jnp.stack(mim, axis=1)
    expo = jnp.stack([rows + 1, SUBLANES - rows]).astype(F32)[..., None]
    cm = jnp.exp((lr * step).reshape(2, 1, ns) * expo)
    ang = (li * step).reshape(2, 1, ns) * expo
    cre, cim = cm * jnp.cos(ang), cm * jnp.sin(ang)
    return bb, cc, mre, mim, cre, cim


def _attn_kernel(tk, lam_ref, q_ref, k_ref, vt_ref, g_ref, o_ref, va_ref, s_ref):
    hd = q_ref.shape[-1]
    dq = hd // 2
    tq = q_ref.shape[2]
    seq = k_ref.shape[2]
    rows_aug = va_ref.shape[0]

    @pl.when(pl.program_id(2) == 0)
    def _():
        va_ref[0:hd, :] = vt_ref[0, 0]
        row = lax.broadcasted_iota(jnp.int32, (rows_aug - hd, seq), 0)
        va_ref[hd:rows_aug, :] = jnp.where(row == 0, 1.0, 0.0).astype(va_ref.dtype)

    q = q_ref[0, 0]
    lane = lax.broadcasted_iota(jnp.int32, q.shape, 1)
    zero = jnp.zeros_like(q)
    qq = jnp.concatenate([jnp.where(lane < dq, q, zero), jnp.where(lane >= dq, q, zero)], axis=0)

    nblk = seq // tk
    kc = min(tk, 2 * LANES)

    def scores(i, slot):
        kb = k_ref[0, 0, pl.ds(pl.multiple_of(i * tk, tk), tk), :]
        s = lax.dot_general(kb, qq, (((1,), (1,)), ((), ())), preferred_element_type=F32)
        s_ref[slot] = s
        return jnp.max(s, axis=0, keepdims=True)

    def softmax_pv(i, slot, m, mb, acc1, acc2):
        m_new = jnp.maximum(m, mb)
        alpha = jnp.exp2(m - m_new)
        d1 = None
        d2 = None
        for c in range(tk // kc):
            p = jnp.exp2(s_ref[slot, c * kc:(c + 1) * kc, :] - m_new).astype(BF16)
            vb = va_ref[:, pl.ds(pl.multiple_of(i * tk + c * kc, kc), kc)]
            t1 = jnp.dot(vb, p[:, 0:tq], preferred_element_type=F32)
            t2 = jnp.dot(vb, p[:, tq:2 * tq], preferred_element_type=F32)
            d1 = t1 if d1 is None else d1 + t1
            d2 = t2 if d2 is None else d2 + t2
        return m_new, acc1 * alpha[:, 0:tq] + d1, acc2 * alpha[:, tq:2 * tq] + d2

    m = jnp.full((1, 2 * tq), -jnp.inf, F32)
    acc1 = jnp.zeros((rows_aug, tq), F32)
    acc2 = acc1
    mb = scores(0, 0)
    if nblk > 1:
        assert nblk % 2 == 0

        def pair(j, carry):
            m, mb, acc1, acc2 = carry
            mb1 = scores(2 * j + 1, 1)
            m, acc1, acc2 = softmax_pv(2 * j, 0, m, mb, acc1, acc2)
            mb2 = scores(2 * j + 2, 0)
            m, acc1, acc2 = softmax_pv(2 * j + 1, 1, m, mb1, acc1, acc2)
            return m, mb2, acc1, acc2

        m, mb, acc1, acc2 = lax.fori_loop(0, nblk // 2 - 1, pair, (m, mb, acc1, acc2))
        mb1 = scores(nblk - 1, 1)
        m, acc1, acc2 = softmax_pv(nblk - 2, 0, m, mb, acc1, acc2)
        m, acc1, acc2 = softmax_pv(nblk - 1, 1, m, mb1, acc1, acc2)
    else:
        m, acc1, acc2 = softmax_pv(0, 0, m, mb, acc1, acc2)
    lam = lam_ref[0]
    post = lam_ref[1]
    o = acc1[0:hd, :] / acc1[hd:hd + 1, :] - lam * (acc2[0:hd, :] / acc2[hd:hd + 1, :])
    o = o * lax.rsqrt(jnp.mean(o * o, axis=0, keepdims=True) + LN_EPS)
    o_ref[0] = (o * g_ref[...] * post).astype(o_ref.dtype)


def _attn_call(lam2, q, k, vt, g_col):
    bsz, nh, seq, hd = q.shape
    tq = min(ATT_Q, seq)
    tk = min(ATT_K, seq)
    rows_aug = hd + 16
    return pl.pallas_call(
        functools.partial(_attn_kernel, tk),
        grid=(bsz, nh, seq // tq),
        in_specs=[pl.BlockSpec(memory_space=pltpu.SMEM),
                  pl.BlockSpec((1, 1, tq, hd), lambda b, h, i: (b, h, i, 0)),
                  pl.BlockSpec((1, 1, seq, hd), lambda b, h, i: (b, h, 0, 0)),
                  pl.BlockSpec((1, 1, hd, seq), lambda b, h, i: (b, h, 0, 0)),
                  _const_spec(g_col.shape)],
        out_specs=pl.BlockSpec((1, hd, tq), lambda b, h, i: (b, h, i)),
        out_shape=jax.ShapeDtypeStruct((bsz, nh * hd, seq), BF16),
        scratch_shapes=[pltpu.VMEM((rows_aug, seq), BF16), pltpu.VMEM((2, tk, 2 * tq), F32)],
        compiler_params=_cparams(("parallel", "parallel", "arbitrary")),
        name="diff_attn",
    )(lam2, q, k, vt, g_col)


def _gelu_tanh(x):
    return 0.5 * x * (1.0 + jnp.tanh(math.sqrt(2.0 / math.pi) * (x + 0.044715 * (x * x * x))))


def _ffn_kernel(alpha, splits, x_ref, ya_ref, u_ref, yf_ref, yb_ref, yc_ref, ydt_ref,
                wo_ref, sd_ref, wg_ref, bg_ref, g1_ref, b1_ref, w1_ref, w3_ref, w2_ref,
                g2_ref, b2_ref, o_ref):
    gw = ya_ref.shape[-1]
    x = x_ref[...]
    ys = _gelu_tanh(sd_ref[...] * u_ref[...] + yf_ref[0] + yb_ref[0])
    gate = jnp.dot(ys.astype(BF16), wg_ref[...], preferred_element_type=F32) + bg_ref[...]
    y_b = (ys * _sigmoid(gate)).astype(BF16)
    mix = jnp.dot(ya_ref[...], wo_ref[0:gw, :], preferred_element_type=F32)
    mix += jnp.dot(y_b, wo_ref[gw:2 * gw, :], preferred_element_type=F32)
    mix += jnp.dot(yc_ref[...], wo_ref[2 * gw:3 * gw, :], preferred_element_type=F32)
    mix += lax.dot_general(ydt_ref[0], wo_ref[3 * gw:4 * gw, :], (((0,), (0,)), ((), ())),
                           preferred_element_type=F32)
    x1 = _layer_norm(alpha * x + mix, g1_ref[...], b1_ref[...])
    x1b = x1.astype(BF16)
    ff = None
    for lo, hi in splits:
        h1 = jnp.dot(x1b, w1_ref[:, lo:hi], preferred_element_type=F32)
        h3 = jnp.dot(x1b, w3_ref[:, lo:hi], preferred_element_type=F32)
        act = (h1 * _sigmoid(h1) * h3).astype(BF16)
        part = jnp.dot(act, w2_ref[lo:hi, :], preferred_element_type=F32)
        ff = part if ff is None else ff + part
    o_ref[...] = _layer_norm(alpha * x1 + ff, g2_ref[...], b2_ref[...])


def _ffn_call(alpha, x2, ya, u, y2, yc, ydt, wo, sd, wg, bgl, g1, b1, w1, w3, w2, g2, b2, bsz, seq):
    t, d = x2.shape
    gw = d // N_MIXERS
    tm = min(FFN_ROWS, seq)
    ns = seq // tm
    hid = w1.shape[-1]
    cut = (hid // 2 + 255) // 256 * 256
    splits = ((0, cut), (cut, hid)) if 0 < cut < hid else ((0, hid),)
    tok = lambda w: pl.BlockSpec((tm, w), lambda i: (i, 0))
    single = lambda a: pl.BlockSpec(a.shape, lambda i: (0,) * a.ndim, pipeline_mode=pl.Buffered(1))
    return pl.pallas_call(
        functools.partial(_ffn_kernel, alpha, splits),
        grid=(t // tm,),
        in_specs=[tok(d), tok(gw), tok(gw),
                  pl.BlockSpec((1, tm, gw), lambda i: (0, i, 0)),
                  pl.BlockSpec((1, tm, gw), lambda i: (1, i, 0)),
                  tok(gw),
                  pl.BlockSpec((1, gw, tm), lambda i: (i // ns, 0, i % ns)),
                  single(wo), single(sd), single(wg), single(bgl), single(g1), single(b1),
                  single(w1), single(w3), single(w2), single(g2), single(b2)],
        out_specs=tok(d),
        out_shape=jax.ShapeDtypeStruct((t, d), F32),
        compiler_params=_cparams(("parallel",)),
        name="out_ffn",
    )(x2, ya, u, y2, y2, yc, ydt, wo, sd, wg, bgl, g1, b1, w1, w3, w2, g2, b2)


def _rope_tables(seq, gw):
    hd = gw // DA_HEADS
    dq = hd // 2
    rot = dq // 4
    half = rot // 2
    pos = jnp.arange(seq, dtype=F32)
    inv_freq = ROPE_THETA ** (-jnp.arange(0, rot, 2, dtype=F32) / rot)
    ang = pos[:, None] * inv_freq[None, :]
    cos, sin = jnp.cos(ang), jnp.sin(ang)
    lane = jnp.arange(gw) % dq
    idx = lane % half
    lo = (lane < half)[None, :]
    hi = ((lane >= half) & (lane < rot))[None, :]
    cos_l, sin_l = cos[:, idx], sin[:, idx]
    rc = jnp.where(lo | hi, cos_l, 1.0)
    ra = jnp.where(lo, -sin_l, 0.0)
    rb = jnp.where(hi, sin_l, 0.0)
    return rc.astype(F32), ra.astype(F32), rb.astype(F32)


def kernel(x, w_in, w_out, conf_dw_w, conf_dw_b, conf_ln_g, conf_ln_b, s5_a_re, s5_a_im, s5_log_step,
           s5_b_re, s5_b_im, s5_c_re, s5_c_im, s5_d, s5_w_glu, s5_b_glu, sc_conv_w, da_lq1, da_lk1,
           da_lq2, da_lk2, da_subln_g, ln1_g, ln1_b, w_ffn1, w_ffn3, w_ffn2, ln2_g, ln2_b):
    bsz, seq, d = x.shape
    depth = w_in.shape[0]
    gw = d // N_MIXERS
    alpha = (2.0 * depth) ** 0.25
    rc, ra, rb = _rope_tables(seq, gw)
    row = lambda a: a.astype(F32).reshape(1, -1)
    x2 = x.reshape(bsz * seq, d)
    for l in range(depth):
        za, u, bg, cv, q, k, vt = _proj_call(x2, w_in[l].astype(BF16), rc, ra, rb, bsz, seq)
        b3 = lambda a: a.reshape(bsz, seq, gw)
        ya, yc = _conv_call(b3(za), b3(cv), b3(bg), conf_dw_w[l].astype(F32), row(conf_dw_b[l]),
                            row(conf_ln_g[l]), row(conf_ln_b[l]), sc_conv_w[l].astype(F32))
        s5p = _s5_params(s5_a_re[l], s5_a_im[l], s5_log_step[l], s5_b_re[l], s5_b_im[l],
                         s5_c_re[l], s5_c_im[l])
        y2 = _s5_call(b3(u), *s5p)
        lam_init = 0.8 - 0.6 * math.exp(-0.3 * l)
        lam = (jnp.exp(jnp.sum(da_lq1[l].astype(F32) * da_lk1[l].astype(F32)))
               - jnp.exp(jnp.sum(da_lq2[l].astype(F32) * da_lk2[l].astype(F32))) + lam_init)
        lam2 = jnp.stack([lam, jnp.asarray(1.0 - lam_init, F32)]).astype(F32)
        ydt = _attn_call(lam2, q, k, vt, da_subln_g[l].astype(F32).reshape(-1, 1))
        x2 = _ffn_call(alpha, x2, ya.reshape(-1, gw), u, y2.reshape(2, -1, gw), yc.reshape(-1, gw), ydt,
                       w_out[l].astype(BF16), row(s5_d[l]), s5_w_glu[l].astype(BF16), row(s5_b_glu[l]),
                       row(ln1_g[l]), row(ln1_b[l]), w_ffn1[l].astype(BF16), w_ffn3[l].astype(BF16),
                       w_ffn2[l].astype(BF16), row(ln2_g[l]), row(ln2_b[l]), bsz, seq)
    return x2.reshape(bsz, seq, d)
```

```python
import functools
import math

import jax
import jax.numpy as jnp
from jax import lax
from jax.experimental import pallas as pl
from jax.experimental.pallas import tpu as pltpu

F32 = jnp.float32
BF16 = jnp.bfloat16

N_MIXERS = 4
CONV_K = 31
S5_CH = 16
S5_STATE = 64
SHORT_K = 3
DA_HEADS = 4
ROPE_THETA = 500000.0
LN_EPS = 1e-5

SUBLANES = 8
LANES = 128
VMEM_LIMIT_BYTES = 56 * 1024 * 1024

PROJ_ROWS = 512
CONV_ROWS = 512
CONV_HALO = 16
CONV_CHUNK = 64
S5_CHUNK = LANES
ATT_Q = 256
ATT_K = 1024
FFN_ROWS = 512


def _cparams(sem):
    return pltpu.CompilerParams(dimension_semantics=sem, vmem_limit_bytes=VMEM_LIMIT_BYTES)


def _const_spec(shape):
    nd = len(shape)
    return pl.BlockSpec(shape, lambda *_: (0,) * nd)


def _sigmoid(x):
    return 1.0 / (1.0 + jnp.exp(-x))


def _layer_norm(x, g, b):
    mu = jnp.mean(x, axis=-1, keepdims=True)
    xc = x - mu
    var = jnp.mean(xc * xc, axis=-1, keepdims=True)
    return xc * lax.rsqrt(var + LN_EPS) * g + b


def _proj_kernel(gw, x_ref, w_ref, rc_ref, ra_ref, rb_ref,
                 za_ref, u_ref, ut_ref, bg_ref, cv_ref, q_ref, k_ref, vt_ref):
    xb = x_ref[...].astype(BF16)

    def seg(i):
        return jnp.dot(xb, w_ref[:, i * gw:(i + 1) * gw], preferred_element_type=F32)

    za_ref[...] = seg(0) * _sigmoid(seg(1))
    u = seg(2)
    u_ref[...] = u
    ut_ref[0] = u.T.astype(BF16)
    bg_ref[...] = seg(3)
    cv_ref[...] = seg(4) * seg(5)

    rc, ra, rb = rc_ref[...], ra_ref[...], rb_ref[...]
    hd = gw // DA_HEADS
    half = hd // 2 // 4 // 2

    def rope(t):
        return t * rc + pltpu.roll(t, gw - half, axis=1) * ra + pltpu.roll(t, half, axis=1) * rb

    scale = (hd // 2) ** -0.5 * math.log2(math.e)
    q = rope(seg(6)) * scale
    k = rope(seg(7))
    vt = seg(8).T
    for h in range(DA_HEADS):
        q_ref[0, h] = q[:, h * hd:(h + 1) * hd].astype(BF16)
        k_ref[0, h] = k[:, h * hd:(h + 1) * hd].astype(BF16)
        vt_ref[0, h] = vt[h * hd:(h + 1) * hd, :].astype(BF16)


def _proj_call(x2, w_in, rc, ra, rb, bsz, seq):
    t, d = x2.shape
    gw = d // N_MIXERS
    tm = min(PROJ_ROWS, seq)
    ns = seq // tm
    hd = gw // DA_HEADS
    tok = lambda: pl.BlockSpec((tm, gw), lambda i: (i, 0))
    rope_spec = lambda: pl.BlockSpec((tm, gw), lambda i: (i % ns, 0))
    out_shape = (
        [jax.ShapeDtypeStruct((t, gw), F32)] * 2
        + [jax.ShapeDtypeStruct((bsz, gw, seq), BF16)]
        + [jax.ShapeDtypeStruct((t, gw), F32)] * 2
        + [jax.ShapeDtypeStruct((bsz, DA_HEADS, seq, hd), BF16)] * 2
        + [jax.ShapeDtypeStruct((bsz, DA_HEADS, hd, seq), BF16)]
    )
    return pl.pallas_call(
        functools.partial(_proj_kernel, gw),
        grid=(t // tm,),
        in_specs=[pl.BlockSpec((tm, d), lambda i: (i, 0)), _const_spec(w_in.shape),
                  rope_spec(), rope_spec(), rope_spec()],
        out_specs=[tok(), tok(), pl.BlockSpec((1, gw, tm), lambda i: (i // ns, 0, i % ns)), tok(), tok(),
                   pl.BlockSpec((1, DA_HEADS, tm, hd), lambda i: (i // ns, 0, i % ns, 0)),
                   pl.BlockSpec((1, DA_HEADS, tm, hd), lambda i: (i // ns, 0, i % ns, 0)),
                   pl.BlockSpec((1, DA_HEADS, hd, tm), lambda i: (i // ns, 0, 0, i % ns))],
        out_shape=out_shape,
        compiler_params=_cparams(("parallel",)),
        name="proj_in",
    )(x2, w_in, rc, ra, rb)


def _conv_kernel(ts, za_ref, zp_ref, zn_ref, cv_ref, cp_ref, cn_ref, bg_ref,
                 dw_ref, db_ref, g_ref, b_ref, sw_ref, ya_ref, yc_ref, zbuf, cbuf):
    j = pl.program_id(1)
    nj = pl.num_programs(1)
    keep_prev = jnp.where(j > 0, 1.0, 0.0).astype(F32)
    keep_next = jnp.where(j < nj - 1, 1.0, 0.0).astype(F32)
    h = CONV_HALO
    zbuf[0:h, :] = zp_ref[0] * keep_prev
    zbuf[h:h + ts, :] = za_ref[0]
    zbuf[h + ts:2 * h + ts, :] = zn_ref[0] * keep_next
    cbuf[0:h, :] = cp_ref[0] * keep_prev
    cbuf[h:h + ts, :] = cv_ref[0]
    cbuf[h + ts:2 * h + ts, :] = cn_ref[0] * keep_next

    rows = CONV_CHUNK
    win_rows = rows + 2 * h

    def taps(buf, base, w_ref, ntaps):
        win = buf[pl.ds(base, win_rows), :]
        first = h - ntaps // 2
        acc = jnp.zeros((rows, win.shape[-1]), F32)
        for phase in range(SUBLANES):
            offs = [o for o in range(first, first + ntaps) if o % SUBLANES == phase]
            if not offs:
                continue
            rolled = win if phase == 0 else pltpu.roll(win, win_rows - phase, axis=0)
            for o in offs:
                lo = o - phase
                acc = acc + w_ref[o - first:o - first + 1, :] * rolled[lo:lo + rows, :]
        return acc

    def chunk(c, carry):
        base = pl.multiple_of(c * rows, rows)
        acc = taps(zbuf, base, dw_ref, CONV_K)
        z = _layer_norm(acc + db_ref[...], g_ref[...], b_ref[...])
        ya_ref[0, pl.ds(base, rows), :] = (z * _sigmoid(z)).astype(ya_ref.dtype)
        sc = taps(cbuf, base, sw_ref, SHORT_K)
        yc_ref[0, pl.ds(base, rows), :] = (bg_ref[0, pl.ds(base, rows), :] * sc).astype(yc_ref.dtype)
        return carry

    lax.fori_loop(0, ts // rows, chunk, 0)


def _conv_call(za, cv, bg, dw_w, dw_b, ln_g, ln_b, sc_w):
    bsz, seq, gw = za.shape
    ts = min(CONV_ROWS, seq)
    r = ts // CONV_HALO
    nh = seq // CONV_HALO
    main = lambda: pl.BlockSpec((1, ts, gw), lambda b, j: (b, j, 0))
    prev = lambda: pl.BlockSpec((1, CONV_HALO, gw), lambda b, j: (b, jnp.maximum(j * r - 1, 0), 0))
    nxt = lambda: pl.BlockSpec((1, CONV_HALO, gw), lambda b, j: (b, jnp.minimum((j + 1) * r, nh - 1), 0))
    return pl.pallas_call(
        functools.partial(_conv_kernel, ts),
        grid=(bsz, seq // ts),
        in_specs=[main(), prev(), nxt(), main(), prev(), nxt(), main(),
                  _const_spec(dw_w.shape), _const_spec(dw_b.shape), _const_spec(ln_g.shape),
                  _const_spec(ln_b.shape), _const_spec(sc_w.shape)],
        out_specs=[main(), main()],
        out_shape=[jax.ShapeDtypeStruct((bsz, seq, gw), BF16)] * 2,
        scratch_shapes=[pltpu.VMEM((ts + 2 * CONV_HALO, gw), F32)] * 2,
        compiler_params=_cparams(("parallel", "parallel")),
        name="dwconv",
    )(za, za, za, cv, cv, cv, bg, dw_w, dw_b, ln_g, ln_b, sc_w)


def _s5_kernel(nc, ut_ref, kv_ref, bc_ref, cc_ref, p1_ref, p2_ref, y_ref, t_ref):
    bsz, pch, _, lc = ut_ref.shape
    rows = bsz * nc
    nst = bc_ref.shape[-1] // 4

    def build(q, carry):
        kvq = kv_ref[0, pl.ds(pl.multiple_of(q * pch, pch), pch), :]
        for p in range(pch):
            x = jnp.broadcast_to(kvq[p:p + 1, :], (lc, 2 * lc))
            tile = pltpu.roll(x, 0, axis=1, stride=1, stride_axis=0)[:, lc:]
            t_ref[pl.ds(pl.multiple_of(q * lc, lc), lc), p * lc:(p + 1) * lc] = tile.astype(t_ref.dtype)
        return carry

    lax.fori_loop(0, pch, build, 0)

    v = jnp.concatenate([ut_ref[:, q].reshape(rows, lc) for q in range(pch)], axis=1)
    y = jnp.dot(v, t_ref[...], preferred_element_type=F32)
    w = jnp.dot(v, bc_ref[0], preferred_element_type=F32)

    cidx = lax.broadcasted_iota(jnp.int32, (bsz, nc, 2 * nst), 1).reshape(rows, 2 * nst)

    def cmul(x, lvl, half):
        lo, hi = half * 2 * nst, (half + 1) * 2 * nst
        return x * p1_ref[0, lvl:lvl + 1, lo:hi] + pltpu.roll(x, nst, axis=1) * p2_ref[0, lvl:lvl + 1, lo:hi]

    def chunk_scan(x, half, reverse):
        lvl, k = 0, 1
        while k < nc:
            sh = pltpu.roll(x, (rows - k) if reverse else k, axis=0)
            keep = (cidx <= nc - 1 - k) if reverse else (cidx >= k)
            x = x + cmul(jnp.where(keep, sh, 0.0), lvl, half)
            lvl, k = lvl + 1, 2 * k
        sh = pltpu.roll(x, (rows - 1) if reverse else 1, axis=0)
        keep = (cidx <= nc - 2) if reverse else (cidx >= 1)
        return jnp.where(keep, sh, 0.0)

    xprev = jnp.concatenate([chunk_scan(w[:, 0:2 * nst], 0, False),
                             chunk_scan(w[:, 2 * nst:4 * nst], 1, True)], axis=1)
    y = y + jnp.dot(xprev.astype(BF16), cc_ref[0], preferred_element_type=F32)
    for p in range(pch):
        y_ref[:, p] = y[:, p * lc:(p + 1) * lc].reshape(bsz, nc, lc)


def _s5_call(ut4, kv, bc, cc, p1, p2):
    bsz, gw, nc, lc = ut4.shape
    ng = kv.shape[0]
    pch = gw // ng
    gspec = lambda a: pl.BlockSpec((1,) + a.shape[1:], lambda g: (g,) + (0,) * (a.ndim - 1))
    io = lambda: pl.BlockSpec((bsz, pch, nc, lc), lambda g: (0, g, 0, 0))
    return pl.pallas_call(
        functools.partial(_s5_kernel, nc),
        grid=(ng,),
        in_specs=[io(), gspec(kv), gspec(bc), gspec(cc), gspec(p1), gspec(p2)],
        out_specs=io(),
        out_shape=jax.ShapeDtypeStruct((bsz, gw, nc, lc), F32),
        scratch_shapes=[pltpu.VMEM((pch * lc, pch * lc), BF16)],
        compiler_params=_cparams(("parallel",)),
        name="s5_mix",
    )(ut4, kv, bc, cc, p1, p2)


def _s5_params(a_re, a_im, log_step, b_re, b_im, c_re, c_im, nc):
    lc = S5_CHUNK
    hi = lax.Precision.HIGHEST
    step = jnp.exp(log_step.astype(F32))[..., None]
    lr, li = a_re.astype(F32), a_im.astype(F32)
    mag = jnp.exp(lr * step)
    abr, abi = mag * jnp.cos(li * step), mag * jnp.sin(li * step)
    den = lr * lr + li * li
    pr = abr - 1.0
    fr = (pr * lr + abi * li) / den
    fi = (abi * lr - pr * li) / den
    br, bi = b_re.astype(F32), b_im.astype(F32)
    bbr = fr[..., None] * br - fi[..., None] * bi
    bbi = fr[..., None] * bi + fi[..., None] * br
    cr, ci = c_re.astype(F32), c_im.astype(F32)
    ls, an = lr * step, li * step

    def apow(ks):
        m = jnp.exp(ls[..., None] * ks)
        return m * jnp.cos(an[..., None] * ks), m * jnp.sin(an[..., None] * ks)

    pw_r, pw_i = apow(jnp.arange(lc + 1, dtype=F32))
    e_r = pw_r[..., None] * bbr[:, :, :, None, :] - pw_i[..., None] * bbi[:, :, :, None, :]
    e_i = pw_r[..., None] * bbi[:, :, :, None, :] + pw_i[..., None] * bbr[:, :, :, None, :]
    kmat = (jnp.einsum('dgpn,dgnkq->dgkpq', cr, e_r, precision=hi)
            - jnp.einsum('dgpn,dgnkq->dgkpq', ci, e_i, precision=hi))
    kf = jnp.transpose(kmat[0, :, :lc], (0, 3, 2, 1))
    kb = jnp.transpose(kmat[1, :, :lc], (0, 3, 2, 1))
    left = jnp.concatenate([jnp.zeros_like(kb[..., :1]), jnp.flip(kb[..., 1:], axis=-1)], axis=-1)
    right = kf.at[..., 0].add(kb[..., 0])
    ng, pch = kf.shape[0], kf.shape[1]
    kv = jnp.concatenate([left, right], axis=-1).reshape(ng, pch * pch, 2 * lc)

    def qs_rows(e):
        return jnp.transpose(e, (0, 3, 2, 1)).reshape(ng, pch * lc, -1)

    bc = jnp.concatenate([qs_rows(jnp.flip(e_r[0, :, :, :lc], axis=2)), qs_rows(jnp.flip(e_i[0, :, :, :lc], axis=2)),
                          qs_rows(e_r[1, :, :, :lc]), qs_rows(e_i[1, :, :, :lc])], axis=-1).astype(BF16)

    def c_rows(d, pr_k, pi_k):
        crn = jnp.transpose(cr[d], (0, 2, 1))[..., None]
        cin = jnp.transpose(ci[d], (0, 2, 1))[..., None]
        re = crn * pr_k[:, :, None, :] - cin * pi_k[:, :, None, :]
        im = crn * pi_k[:, :, None, :] + cin * pr_k[:, :, None, :]
        return re.reshape(ng, -1, pch * lc), im.reshape(ng, -1, pch * lc)

    cf_r, cf_i = c_rows(0, pw_r[0, :, :, 1:lc + 1], pw_i[0, :, :, 1:lc + 1])
    cb_r, cb_i = c_rows(1, jnp.flip(pw_r[1, :, :, 1:lc + 1], axis=-1), jnp.flip(pw_i[1, :, :, 1:lc + 1], axis=-1))
    cc = jnp.concatenate([cf_r, -cf_i, cb_r, -cb_i], axis=1).astype(BF16)

    nlev = max(1, (nc - 1).bit_length())
    lv_r, lv_i = apow(float(lc) * 2.0 ** jnp.arange(nlev, dtype=F32))
    lv_r, lv_i = jnp.transpose(lv_r, (0, 1, 3, 2)), jnp.transpose(lv_i, (0, 1, 3, 2))
    p1 = jnp.concatenate([lv_r[0], lv_r[0], lv_r[1], lv_r[1]], axis=-1)
    p2 = jnp.concatenate([-lv_i[0], lv_i[0], -lv_i[1], lv_i[1]], axis=-1)
    return kv, bc, cc, p1, p2


def _attn_kernel(tk, lam_ref, q_ref, k_ref, vt_ref, g_ref, o_ref, va_ref, s_ref):
    hd = q_ref.shape[-1]
    dq = hd // 2
    tq = q_ref.shape[2]
    seq = k_ref.shape[2]
    rows_aug = va_ref.shape[0]

    @pl.when(pl.program_id(2) == 0)
    def _():
        va_ref[0:hd, :] = vt_ref[0, 0]
        row = lax.broadcasted_iota(jnp.int32, (rows_aug - hd, seq), 0)
        va_ref[hd:rows_aug, :] = jnp.where(row == 0, 1.0, 0.0).astype(va_ref.dtype)

    q = q_ref[0, 0]
    lane = lax.broadcasted_iota(jnp.int32, q.shape, 1)
    zero = jnp.zeros_like(q)
    qq = jnp.concatenate([jnp.where(lane < dq, q, zero), jnp.where(lane >= dq, q, zero)], axis=0)

    nblk = seq // tk
    kc = min(tk, 2 * LANES)

    def scores(i, slot):
        kb = k_ref[0, 0, pl.ds(pl.multiple_of(i * tk, tk), tk), :]
        s = lax.dot_general(kb, qq, (((1,), (1,)), ((), ())), preferred_element_type=F32)
        s_ref[slot] = s
        return jnp.max(s, axis=0, keepdims=True)

    def softmax_pv(i, slot, m, mb, acc1, acc2):
        m_new = jnp.maximum(m, mb)
        alpha = jnp.exp2(m - m_new)
        d1 = None
        d2 = None
        for c in range(tk // kc):
            p = jnp.exp2(s_ref[slot, c * kc:(c + 1) * kc, :] - m_new).astype(BF16)
            vb = va_ref[:, pl.ds(pl.multiple_of(i * tk + c * kc, kc), kc)]
            t1 = jnp.dot(vb, p[:, 0:tq], preferred_element_type=F32)
            t2 = jnp.dot(vb, p[:, tq:2 * tq], preferred_element_type=F32)
            d1 = t1 if d1 is None else d1 + t1
            d2 = t2 if d2 is None else d2 + t2
        return m_new, acc1 * alpha[:, 0:tq] + d1, acc2 * alpha[:, tq:2 * tq] + d2

    m = jnp.full((1, 2 * tq), -jnp.inf, F32)
    acc1 = jnp.zeros((rows_aug, tq), F32)
    acc2 = acc1
    mb = scores(0, 0)
    if nblk > 1:
        assert nblk % 2 == 0

        def pair(j, carry):
            m, mb, acc1, acc2 = carry
            mb1 = scores(2 * j + 1, 1)
            m, acc1, acc2 = softmax_pv(2 * j, 0, m, mb, acc1, acc2)
            mb2 = scores(2 * j + 2, 0)
            m, acc1, acc2 = softmax_pv(2 * j + 1, 1, m, mb1, acc1, acc2)
            return m, mb2, acc1, acc2

        m, mb, acc1, acc2 = lax.fori_loop(0, nblk // 2 - 1, pair, (m, mb, acc1, acc2))
        mb1 = scores(nblk - 1, 1)
        m, acc1, acc2 = softmax_pv(nblk - 2, 0, m, mb, acc1, acc2)
        m, acc1, acc2 = softmax_pv(nblk - 1, 1, m, mb1, acc1, acc2)
    else:
        m, acc1, acc2 = softmax_pv(0, 0, m, mb, acc1, acc2)
    lam = lam_ref[0]
    post = lam_ref[1]
    o = acc1[0:hd, :] / acc1[hd:hd + 1, :] - lam * (acc2[0:hd, :] / acc2[hd:hd + 1, :])
    o = o * lax.rsqrt(jnp.mean(o * o, axis=0, keepdims=True) + LN_EPS)
    o_ref[0] = (o * g_ref[...] * post).astype(o_ref.dtype)


def _attn_call(lam2, q, k, vt, g_col):
    bsz, nh, seq, hd = q.shape
    tq = min(ATT_Q, seq)
    tk = min(ATT_K, seq)
    rows_aug = hd + 16
    return pl.pallas_call(
        functools.partial(_attn_kernel, tk),
        grid=(bsz, nh, seq // tq),
        in_specs=[pl.BlockSpec(memory_space=pltpu.SMEM),
                  pl.BlockSpec((1, 1, tq, hd), lambda b, h, i: (b, h, i, 0)),
                  pl.BlockSpec((1, 1, seq, hd), lambda b, h, i: (b, h, 0, 0)),
                  pl.BlockSpec((1, 1, hd, seq), lambda b, h, i: (b, h, 0, 0)),
                  _const_spec(g_col.shape)],
        out_specs=pl.BlockSpec((1, hd, tq), lambda b, h, i: (b, h, i)),
        out_shape=jax.ShapeDtypeStruct((bsz, nh * hd, seq), BF16),
        scratch_shapes=[pltpu.VMEM((rows_aug, seq), BF16), pltpu.VMEM((2, tk, 2 * tq), F32)],
        compiler_params=_cparams(("parallel", "parallel", "arbitrary")),
        name="diff_attn",
    )(lam2, q, k, vt, g_col)


def _gelu_tanh(x):
    return 0.5 * x * (1.0 + jnp.tanh(math.sqrt(2.0 / math.pi) * (x + 0.044715 * (x * x * x))))


def _ffn_kernel(alpha, splits, x_ref, ya_ref, u_ref, yst_ref, yc_ref, ydt_ref,
                wo_ref, sd_ref, wg_ref, bg_ref, g1_ref, b1_ref, w1_ref, w3_ref, w2_ref,
                g2_ref, b2_ref, o_ref):
    gw = ya_ref.shape[-1]
    x = x_ref[...]
    ys = _gelu_tanh(sd_ref[...] * u_ref[...] + yst_ref[0].T)
    gate = jnp.dot(ys.astype(BF16), wg_ref[...], preferred_element_type=F32) + bg_ref[...]
    y_b = (ys * _sigmoid(gate)).astype(BF16)
    mix = jnp.dot(ya_ref[...], wo_ref[0:gw, :], preferred_element_type=F32)
    mix += jnp.dot(y_b, wo_ref[gw:2 * gw, :], preferred_element_type=F32)
    mix += jnp.dot(yc_ref[...], wo_ref[2 * gw:3 * gw, :], preferred_element_type=F32)
    mix += lax.dot_general(ydt_ref[0], wo_ref[3 * gw:4 * gw, :], (((0,), (0,)), ((), ())),
                           preferred_element_type=F32)
    x1 = _layer_norm(alpha * x + mix, g1_ref[...], b1_ref[...])
    x1b = x1.astype(BF16)
    ff = None
    for lo, hi in splits:
        h1 = jnp.dot(x1b, w1_ref[:, lo:hi], preferred_element_type=F32)
        h3 = jnp.dot(x1b, w3_ref[:, lo:hi], preferred_element_type=F32)
        act = (h1 * _sigmoid(h1) * h3).astype(BF16)
        part = jnp.dot(act, w2_ref[lo:hi, :], preferred_element_type=F32)
        ff = part if ff is None else ff + part
    o_ref[...] = _layer_norm(alpha * x1 + ff, g2_ref[...], b2_ref[...])


def _ffn_call(alpha, x2, ya, u, yst, yc, ydt, wo, sd, wg, bgl, g1, b1, w1, w3, w2, g2, b2, bsz, seq):
    t, d = x2.shape
    gw = d // N_MIXERS
    tm = min(FFN_ROWS, seq)
    ns = seq // tm
    hid = w1.shape[-1]
    cut = (hid // 2 + 255) // 256 * 256
    splits = ((0, cut), (cut, hid)) if 0 < cut < hid else ((0, hid),)
    tok = lambda w: pl.BlockSpec((tm, w), lambda i: (i, 0))
    single = lambda a: pl.BlockSpec(a.shape, lambda i: (0,) * a.ndim, pipeline_mode=pl.Buffered(1))
    return pl.pallas_call(
        functools.partial(_ffn_kernel, alpha, splits),
        grid=(t // tm,),
        in_specs=[tok(d), tok(gw), tok(gw),
                  pl.BlockSpec((1, gw, tm), lambda i: (i // ns, 0, i % ns)),
                  tok(gw),
                  pl.BlockSpec((1, gw, tm), lambda i: (i // ns, 0, i % ns)),
                  single(wo), single(sd), single(wg), single(bgl), single(g1), single(b1),
                  single(w1), single(w3), single(w2), single(g2), single(b2)],
        out_specs=tok(d),
        out_shape=jax.ShapeDtypeStruct((t, d), F32),
        compiler_params=_cparams(("parallel",)),
        name="out_ffn",
    )(x2, ya, u, yst, yc, ydt, wo, sd, wg, bgl, g1, b1, w1, w3, w2, g2, b2)


def _rope_tables(seq, gw):
    hd = gw // DA_HEADS
    dq = hd // 2
    rot = dq // 4
    half = rot // 2
    pos = jnp.arange(seq, dtype=F32)
    inv_freq = ROPE_THETA ** (-jnp.arange(0, rot, 2, dtype=F32) / rot)
    ang = pos[:, None] * inv_freq[None, :]
    cos, sin = jnp.cos(ang), jnp.sin(ang)
    lane = jnp.arange(gw) % dq
    idx = lane % half
    lo = (lane < half)[None, :]
    hi = ((lane >= half) & (lane < rot))[None, :]
    cos_l, sin_l = cos[:, idx], sin[:, idx]
    rc = jnp.where(lo | hi, cos_l, 1.0)
    ra = jnp.where(lo, -sin_l, 0.0)
    rb = jnp.where(hi, sin_l, 0.0)
    return rc.astype(F32), ra.astype(F32), rb.astype(F32)


def kernel(x, w_in, w_out, conf_dw_w, conf_dw_b, conf_ln_g, conf_ln_b, s5_a_re, s5_a_im, s5_log_step,
           s5_b_re, s5_b_im, s5_c_re, s5_c_im, s5_d, s5_w_glu, s5_b_glu, sc_conv_w, da_lq1, da_lk1,
           da_lq2, da_lk2, da_subln_g, ln1_g, ln1_b, w_ffn1, w_ffn3, w_ffn2, ln2_g, ln2_b):
    bsz, seq, d = x.shape
    depth = w_in.shape[0]
    gw = d // N_MIXERS
    alpha = (2.0 * depth) ** 0.25
    rc, ra, rb = _rope_tables(seq, gw)
    row = lambda a: a.astype(F32).reshape(1, -1)
    x2 = x.reshape(bsz * seq, d)
    for l in range(depth):
        za, u, ut, bg, cv, q, k, vt = _proj_call(x2, w_in[l].astype(BF16), rc, ra, rb, bsz, seq)
        b3 = lambda a: a.reshape(bsz, seq, gw)
        ya, yc = _conv_call(b3(za), b3(cv), b3(bg), conf_dw_w[l].astype(F32), row(conf_dw_b[l]),
                            row(conf_ln_g[l]), row(conf_ln_b[l]), sc_conv_w[l].astype(F32))
        nc = seq // S5_CHUNK
        s5p = _s5_params(s5_a_re[l], s5_a_im[l], s5_log_step[l], s5_b_re[l], s5_b_im[l],
                         s5_c_re[l], s5_c_im[l], nc)
        yst = _s5_call(ut.reshape(bsz, gw, nc, S5_CHUNK), *s5p).reshape(bsz, gw, seq)
        lam_init = 0.8 - 0.6 * math.exp(-0.3 * l)
        lam = (jnp.exp(jnp.sum(da_lq1[l].astype(F32) * da_lk1[l].astype(F32)))
               - jnp.exp(jnp.sum(da_lq2[l].astype(F32) * da_lk2[l].astype(F32))) + lam_init)
        lam2 = jnp.stack([lam, jnp.asarray(1.0 - lam_init, F32)]).astype(F32)
        ydt = _attn_call(lam2, q, k, vt, da_subln_g[l].astype(F32).reshape(-1, 1))
        x2 = _ffn_call(alpha, x2, ya.reshape(-1, gw), u, yst, yc.reshape(-1, gw), ydt,
                       w_out[l].astype(BF16), row(s5_d[l]), s5_w_glu[l].astype(BF16), row(s5_b_glu[l]),
                       row(ln1_g[l]), row(ln1_b[l]), w_ffn1[l].astype(BF16), w_ffn3[l].astype(BF16),
                       w_ffn2[l].astype(BF16), row(ln2_g[l]), row(ln2_b[l]), bsz, seq)
    return x2.reshape(bsz, seq, d)
```

```python
import functools
import math

import jax
import jax.numpy as jnp
from jax import lax
from jax.experimental import pallas as pl
from jax.experimental.pallas import tpu as pltpu

F32 = jnp.float32
BF16 = jnp.bfloat16

N_MIXERS = 4
CONV_K = 31
S5_CH = 16
S5_STATE = 64
SHORT_K = 3
DA_HEADS = 4
ROPE_THETA = 500000.0
LN_EPS = 1e-5

SUBLANES = 8
LANES = 128
VMEM_LIMIT_BYTES = 56 * 1024 * 1024

PROJ_ROWS = 512
CONV_ROWS = 512
CONV_HALO = 16
CONV_CHUNK = 64
S5_CHUNK = LANES
ATT_Q = 256
ATT_K = 1024
FFN_ROWS = 512


def _cparams(sem):
    return pltpu.CompilerParams(dimension_semantics=sem, vmem_limit_bytes=VMEM_LIMIT_BYTES)


def _const_spec(shape):
    nd = len(shape)
    return pl.BlockSpec(shape, lambda *_: (0,) * nd)


def _sigmoid(x):
    return 1.0 / (1.0 + jnp.exp(-x))


def _layer_norm(x, g, b):
    mu = jnp.mean(x, axis=-1, keepdims=True)
    xc = x - mu
    var = jnp.mean(xc * xc, axis=-1, keepdims=True)
    return xc * lax.rsqrt(var + LN_EPS) * g + b


def _proj_kernel(gw, x_ref, w_ref, rc_ref, ra_ref, rb_ref,
                 za_ref, u_ref, ut_ref, bg_ref, cv_ref, q_ref, k_ref, vt_ref):
    xb = x_ref[...].astype(BF16)

    def seg(i):
        return jnp.dot(xb, w_ref[:, i * gw:(i + 1) * gw], preferred_element_type=F32)

    za_ref[...] = seg(0) * _sigmoid(seg(1))
    u = seg(2)
    u_ref[...] = u
    ut_ref[0] = u.T.astype(BF16)
    bg_ref[...] = seg(3)
    cv_ref[...] = seg(4) * seg(5)

    rc, ra, rb = rc_ref[...], ra_ref[...], rb_ref[...]
    hd = gw // DA_HEADS
    half = hd // 2 // 4 // 2

    def rope(t):
        return t * rc + pltpu.roll(t, gw - half, axis=1) * ra + pltpu.roll(t, half, axis=1) * rb

    scale = (hd // 2) ** -0.5 * math.log2(math.e)
    q = rope(seg(6)) * scale
    k = rope(seg(7))
    vt = seg(8).T
    for h in range(DA_HEADS):
        q_ref[0, h] = q[:, h * hd:(h + 1) * hd].astype(BF16)
        k_ref[0, h] = k[:, h * hd:(h + 1) * hd].astype(BF16)
        vt_ref[0, h] = vt[h * hd:(h + 1) * hd, :].astype(BF16)


def _proj_call(x2, w_in, rc, ra, rb, bsz, seq):
    t, d = x2.shape
    gw = d // N_MIXERS
    tm = min(PROJ_ROWS, seq)
    ns = seq // tm
    hd = gw // DA_HEADS
    tok = lambda: pl.BlockSpec((tm, gw), lambda i: (i, 0))
    rope_spec = lambda: pl.BlockSpec((tm, gw), lambda i: (i % ns, 0))
    out_shape = (
        [jax.ShapeDtypeStruct((t, gw), F32)] * 2
        + [jax.ShapeDtypeStruct((bsz, gw, seq), BF16)]
        + [jax.ShapeDtypeStruct((t, gw), F32)] * 2
        + [jax.ShapeDtypeStruct((bsz, DA_HEADS, seq, hd), BF16)] * 2
        + [jax.ShapeDtypeStruct((bsz, DA_HEADS, hd, seq), BF16)]
    )
    return pl.pallas_call(
        functools.partial(_proj_kernel, gw),
        grid=(t // tm,),
        in_specs=[pl.BlockSpec((tm, d), lambda i: (i, 0)), _const_spec(w_in.shape),
                  rope_spec(), rope_spec(), rope_spec()],
        out_specs=[tok(), tok(), pl.BlockSpec((1, gw, tm), lambda i: (i // ns, 0, i % ns)), tok(), tok(),
                   pl.BlockSpec((1, DA_HEADS, tm, hd), lambda i: (i // ns, 0, i % ns, 0)),
                   pl.BlockSpec((1, DA_HEADS, tm, hd), lambda i: (i // ns, 0, i % ns, 0)),
                   pl.BlockSpec((1, DA_HEADS, hd, tm), lambda i: (i // ns, 0, 0, i % ns))],
        out_shape=out_shape,
        compiler_params=_cparams(("parallel",)),
        name="proj_in",
    )(x2, w_in, rc, ra, rb)


def _conv_kernel(ts, za_ref, zp_ref, zn_ref, cv_ref, cp_ref, cn_ref, bg_ref,
                 dw_ref, db_ref, g_ref, b_ref, sw_ref, ya_ref, yc_ref, zbuf, cbuf):
    j = pl.program_id(1)
    nj = pl.num_programs(1)
    keep_prev = jnp.where(j > 0, 1.0, 0.0).astype(F32)
    keep_next = jnp.where(j < nj - 1, 1.0, 0.0).astype(F32)
    h = CONV_HALO
    zbuf[0:h, :] = zp_ref[0] * keep_prev
    zbuf[h:h + ts, :] = za_ref[0]
    zbuf[h + ts:2 * h + ts, :] = zn_ref[0] * keep_next
    cbuf[0:h, :] = cp_ref[0] * keep_prev
    cbuf[h:h + ts, :] = cv_ref[0]
    cbuf[h + ts:2 * h + ts, :] = cn_ref[0] * keep_next

    rows = CONV_CHUNK
    win_rows = rows + 2 * h

    def taps(buf, base, w_ref, ntaps):
        win = buf[pl.ds(base, win_rows), :]
        first = h - ntaps // 2
        acc = jnp.zeros((rows, win.shape[-1]), F32)
        for phase in range(SUBLANES):
            offs = [o for o in range(first, first + ntaps) if o % SUBLANES == phase]
            if not offs:
                continue
            rolled = win if phase == 0 else pltpu.roll(win, win_rows - phase, axis=0)
            for o in offs:
                lo = o - phase
                acc = acc + w_ref[o - first:o - first + 1, :] * rolled[lo:lo + rows, :]
        return acc

    def chunk(c, carry):
        base = pl.multiple_of(c * rows, rows)
        acc = taps(zbuf, base, dw_ref, CONV_K)
        z = _layer_norm(acc + db_ref[...], g_ref[...], b_ref[...])
        ya_ref[0, pl.ds(base, rows), :] = (z * _sigmoid(z)).astype(ya_ref.dtype)
        sc = taps(cbuf, base, sw_ref, SHORT_K)
        yc_ref[0, pl.ds(base, rows), :] = (bg_ref[0, pl.ds(base, rows), :] * sc).astype(yc_ref.dtype)
        return carry

    lax.fori_loop(0, ts // rows, chunk, 0)


def _conv_call(za, cv, bg, dw_w, dw_b, ln_g, ln_b, sc_w):
    bsz, seq, gw = za.shape
    ts = min(CONV_ROWS, seq)
    r = ts // CONV_HALO
    nh = seq // CONV_HALO
    main = lambda: pl.BlockSpec((1, ts, gw), lambda b, j: (b, j, 0))
    prev = lambda: pl.BlockSpec((1, CONV_HALO, gw), lambda b, j: (b, jnp.maximum(j * r - 1, 0), 0))
    nxt = lambda: pl.BlockSpec((1, CONV_HALO, gw), lambda b, j: (b, jnp.minimum((j + 1) * r, nh - 1), 0))
    return pl.pallas_call(
        functools.partial(_conv_kernel, ts),
        grid=(bsz, seq // ts),
        in_specs=[main(), prev(), nxt(), main(), prev(), nxt(), main(),
                  _const_spec(dw_w.shape), _const_spec(dw_b.shape), _const_spec(ln_g.shape),
                  _const_spec(ln_b.shape), _const_spec(sc_w.shape)],
        out_specs=[main(), main()],
        out_shape=[jax.ShapeDtypeStruct((bsz, seq, gw), BF16)] * 2,
        scratch_shapes=[pltpu.VMEM((ts + 2 * CONV_HALO, gw), F32)] * 2,
        compiler_params=_cparams(("parallel", "parallel")),
        name="dwconv",
    )(za, za, za, cv, cv, cv, bg, dw_w, dw_b, ln_g, ln_b, sc_w)


def _s5_kernel(nc, ut_ref, kv_ref, bc_ref, cc_ref, p1_ref, p2_ref, y_ref, t_ref):
    bsz, pch, _, lc = ut_ref.shape
    rows = bsz * nc
    nst = bc_ref.shape[-1] // 4

    def build(q, carry):
        kvq = kv_ref[0, pl.ds(pl.multiple_of(q * pch, pch), pch), :]
        for p in range(pch):
            x = jnp.broadcast_to(kvq[p:p + 1, :], (lc, 2 * lc))
            tile = pltpu.roll(x, 0, axis=1, stride=1, stride_axis=0)[:, lc:]
            t_ref[pl.ds(pl.multiple_of(q * lc, lc), lc), p * lc:(p + 1) * lc] = tile.astype(t_ref.dtype)
        return carry

    lax.fori_loop(0, pch, build, 0)

    v = jnp.concatenate([ut_ref[:, q].reshape(rows, lc) for q in range(pch)], axis=1)
    y = jnp.dot(v, t_ref[...], preferred_element_type=F32)
    w = jnp.dot(v, bc_ref[0], preferred_element_type=F32)

    cidx = lax.broadcasted_iota(jnp.int32, (bsz, nc, 2 * nst), 1).reshape(rows, 2 * nst)

    def cmul(x, lvl, half):
        lo, hi = half * 2 * nst, (half + 1) * 2 * nst
        return x * p1_ref[0, lvl:lvl + 1, lo:hi] + pltpu.roll(x, nst, axis=1) * p2_ref[0, lvl:lvl + 1, lo:hi]

    def chunk_scan(x, half, reverse):
        lvl, k = 0, 1
        while k < nc:
            sh = pltpu.roll(x, (rows - k) if reverse else k, axis=0)
            keep = (cidx <= nc - 1 - k) if reverse else (cidx >= k)
            x = x + cmul(jnp.where(keep, sh, 0.0), lvl, half)
            lvl, k = lvl + 1, 2 * k
        sh = pltpu.roll(x, (rows - 1) if reverse else 1, axis=0)
        keep = (cidx <= nc - 2) if reverse else (cidx >= 1)
        return jnp.where(keep, sh, 0.0)

    xprev = jnp.concatenate([chunk_scan(w[:, 0:2 * nst], 0, False),
                             chunk_scan(w[:, 2 * nst:4 * nst], 1, True)], axis=1)
    y = y + jnp.dot(xprev.astype(BF16), cc_ref[0], preferred_element_type=F32)
    for p in range(pch):
        y_ref[:, p] = y[:, p * lc:(p + 1) * lc].reshape(bsz, nc, lc)


def _s5_call(ut4, kv, bc, cc, p1, p2):
    bsz, gw, nc, lc = ut4.shape
    ng = kv.shape[0]
    pch = gw // ng
    gspec = lambda a: pl.BlockSpec((1,) + a.shape[1:], lambda g: (g,) + (0,) * (a.ndim - 1))
    io = lambda: pl.BlockSpec((bsz, pch, nc, lc), lambda g: (0, g, 0, 0))
    return pl.pallas_call(
        functools.partial(_s5_kernel, nc),
        grid=(ng,),
        in_specs=[io(), gspec(kv), gspec(bc), gspec(cc), gspec(p1), gspec(p2)],
        out_specs=io(),
        out_shape=jax.ShapeDtypeStruct((bsz, gw, nc, lc), F32),
        scratch_shapes=[pltpu.VMEM((pch * lc, pch * lc), BF16)],
        compiler_params=_cparams(("parallel",)),
        name="s5_mix",
    )(ut4, kv, bc, cc, p1, p2)


def _s5_params(a_re, a_im, log_step, b_re, b_im, c_re, c_im, nc):
    lc = S5_CHUNK
    hi = lax.Precision.HIGHEST
    step = jnp.exp(log_step.astype(F32))[..., None]
    lr, li = a_re.astype(F32), a_im.astype(F32)
    mag = jnp.exp(lr * step)
    abr, abi = mag * jnp.cos(li * step), mag * jnp.sin(li * step)
    den = lr * lr + li * li
    pr = abr - 1.0
    fr = (pr * lr + abi * li) / den
    fi = (abi * lr - pr * li) / den
    br, bi = b_re.astype(F32), b_im.astype(F32)
    bbr = fr[..., None] * br - fi[..., None] * bi
    bbi = fr[..., None] * bi + fi[..., None] * br
    cr, ci = c_re.astype(F32), c_im.astype(F32)
    ls, an = lr * step, li * step
    nly, _, ng, nst = ls.shape
    pch = br.shape[-1]

    def apow(ks):
        m = jnp.exp(ls[..., None] * ks)
        return m * jnp.cos(an[..., None] * ks), m * jnp.sin(an[..., None] * ks)

    mr, mi = apow(jnp.arange(lc + 1, dtype=F32))
    mrt, mit = jnp.swapaxes(mr, -1, -2), jnp.swapaxes(mi, -1, -2)

    def lag_kernels(d):
        pr_k, pi_k = mr[:, d, :, :, None, :lc], mi[:, d, :, :, None, :lc]
        b_r, b_i = bbr[:, d, :, :, :, None], bbi[:, d, :, :, :, None]
        e_r = pr_k * b_r - pi_k * b_i
        e_i = pr_k * b_i + pi_k * b_r
        return (jnp.einsum('lgpn,lgnqk->lgqpk', cr[:, d], e_r, precision=hi)
                - jnp.einsum('lgpn,lgnqk->lgqpk', ci[:, d], e_i, precision=hi))

    kf, kb = lag_kernels(0), lag_kernels(1)
    left = jnp.concatenate([jnp.zeros_like(kb[..., :1]), jnp.flip(kb[..., 1:], axis=-1)], axis=-1)
    right = kf.at[..., 0].add(kb[..., 0])
    kv = jnp.concatenate([left, right], axis=-1).reshape(nly, ng, pch * pch, 2 * lc)

    f_r, f_i = jnp.flip(mrt[:, 0, :, :lc], axis=2), jnp.flip(mit[:, 0, :, :lc], axis=2)
    b_r, b_i = mrt[:, 1, :, :lc], mit[:, 1, :, :lc]
    pa = jnp.concatenate([f_r, f_i, b_r, b_i], axis=-1)[:, :, None]
    pb = jnp.concatenate([-f_i, f_r, -b_i, b_r], axis=-1)[:, :, None]
    bt_r, bt_i = jnp.swapaxes(bbr, -1, -2), jnp.swapaxes(bbi, -1, -2)
    bq1 = jnp.concatenate([bt_r[:, 0], bt_r[:, 0], bt_r[:, 1], bt_r[:, 1]], axis=-1)[:, :, :, None]
    bq2 = jnp.concatenate([bt_i[:, 0], bt_i[:, 0], bt_i[:, 1], bt_i[:, 1]], axis=-1)[:, :, :, None]
    bc = (pa * bq1 + pb * bq2).reshape(nly, ng, pch * lc, 4 * nst).astype(BF16)

    qf_r, qf_i = mr[:, 0, :, :, 1:lc + 1], mi[:, 0, :, :, 1:lc + 1]
    qb_r, qb_i = jnp.flip(mr[:, 1, :, :, 1:lc + 1], axis=-1), jnp.flip(mi[:, 1, :, :, 1:lc + 1], axis=-1)
    qa = jnp.concatenate([qf_r, -qf_i, qb_r, -qb_i], axis=2)[:, :, :, None]
    qb = jnp.concatenate([-qf_i, -qf_r, -qb_i, -qb_r], axis=2)[:, :, :, None]
    ct_r, ct_i = jnp.swapaxes(cr, -1, -2), jnp.swapaxes(ci, -1, -2)
    c1 = jnp.concatenate([ct_r[:, 0], ct_r[:, 0], ct_r[:, 1], ct_r[:, 1]], axis=2)[..., None]
    c2 = jnp.concatenate([ct_i[:, 0], ct_i[:, 0], ct_i[:, 1], ct_i[:, 1]], axis=2)[..., None]
    cc = (qa * c1 + qb * c2).reshape(nly, ng, 4 * nst, pch * lc).astype(BF16)

    nlev = max(1, (nc - 1).bit_length())
    lv_r, lv_i = apow(float(lc) * 2.0 ** jnp.arange(nlev, dtype=F32))
    lv_r, lv_i = jnp.swapaxes(lv_r, -1, -2), jnp.swapaxes(lv_i, -1, -2)
    p1 = jnp.concatenate([lv_r[:, 0], lv_r[:, 0], lv_r[:, 1], lv_r[:, 1]], axis=-1)
    p2 = jnp.concatenate([-lv_i[:, 0], lv_i[:, 0], -lv_i[:, 1], lv_i[:, 1]], axis=-1)
    return kv, bc, cc, p1, p2


def _attn_kernel(tk, lam_ref, q_ref, qn_ref, k_ref, vt_ref, g_ref, o_ref, va_ref, s_ref, mb_ref):
    hd = q_ref.shape[-1]
    dq = hd // 2
    tq = q_ref.shape[2]
    seq = k_ref.shape[2]
    rows_aug = va_ref.shape[0]

    def split_maps(q):
        lane = lax.broadcasted_iota(jnp.int32, q.shape, 1)
        zero = jnp.zeros_like(q)
        return jnp.concatenate([jnp.where(lane < dq, q, zero), jnp.where(lane >= dq, q, zero)], axis=0)

    nblk = seq // tk
    kc = min(tk, 2 * LANES)

    def scores(qq, i, slot):
        kb = k_ref[0, 0, pl.ds(pl.multiple_of(i * tk, tk), tk), :]
        s = lax.dot_general(kb, qq, (((1,), (1,)), ((), ())), preferred_element_type=F32)
        s_ref[slot] = s
        return jnp.max(s, axis=0, keepdims=True)

    qq = split_maps(q_ref[0, 0])

    @pl.when(pl.program_id(2) == 0)
    def _():
        va_ref[0:hd, :] = vt_ref[0, 0]
        row = lax.broadcasted_iota(jnp.int32, (rows_aug - hd, seq), 0)
        va_ref[hd:rows_aug, :] = jnp.where(row == 0, 1.0, 0.0).astype(va_ref.dtype)
        mb_ref[...] = scores(qq, 0, 0)

    def softmax_pv(i, slot, m, mb, acc1, acc2):
        m_new = jnp.maximum(m, mb)
        alpha = jnp.exp2(m - m_new)
        d1 = None
        d2 = None
        for c in range(tk // kc):
            p = jnp.exp2(s_ref[slot, c * kc:(c + 1) * kc, :] - m_new).astype(BF16)
            vb = va_ref[:, pl.ds(pl.multiple_of(i * tk + c * kc, kc), kc)]
            t1 = jnp.dot(vb, p[:, 0:tq], preferred_element_type=F32)
            t2 = jnp.dot(vb, p[:, tq:2 * tq], preferred_element_type=F32)
            d1 = t1 if d1 is None else d1 + t1
            d2 = t2 if d2 is None else d2 + t2
        return m_new, acc1 * alpha[:, 0:tq] + d1, acc2 * alpha[:, tq:2 * tq] + d2

    m = jnp.full((1, 2 * tq), -jnp.inf, F32)
    acc1 = jnp.zeros((rows_aug, tq), F32)
    acc2 = acc1
    assert nblk % 2 == 0
    mb = mb_ref[...]

    def pair(j, carry):
        m, mb, acc1, acc2 = carry
        mb1 = scores(qq, 2 * j + 1, 1)
        m, acc1, acc2 = softmax_pv(2 * j, 0, m, mb, acc1, acc2)
        mb2 = scores(qq, 2 * j + 2, 0)
        m, acc1, acc2 = softmax_pv(2 * j + 1, 1, m, mb1, acc1, acc2)
        return m, mb2, acc1, acc2

    m, mb, acc1, acc2 = lax.fori_loop(0, nblk // 2 - 1, pair, (m, mb, acc1, acc2))
    mb1 = scores(qq, nblk - 1, 1)
    m, acc1, acc2 = softmax_pv(nblk - 2, 0, m, mb, acc1, acc2)
    mb_ref[...] = scores(split_maps(qn_ref[0, 0]), 0, 0)
    m, acc1, acc2 = softmax_pv(nblk - 1, 1, m, mb1, acc1, acc2)
    lam = lam_ref[0]
    post = lam_ref[1]
    o = acc1[0:hd, :] / acc1[hd:hd + 1, :] - lam * (acc2[0:hd, :] / acc2[hd:hd + 1, :])
    o = o * lax.rsqrt(jnp.mean(o * o, axis=0, keepdims=True) + LN_EPS)
    o_ref[0] = (o * g_ref[...] * post).astype(o_ref.dtype)


def _attn_call(lam2, q, k, vt, g_col):
    bsz, nh, seq, hd = q.shape
    tq = min(ATT_Q, seq)
    tk = min(ATT_K, seq)
    rows_aug = hd + 16
    nq = seq // tq
    return pl.pallas_call(
        functools.partial(_attn_kernel, tk),
        grid=(bsz, nh, nq),
        in_specs=[pl.BlockSpec(memory_space=pltpu.SMEM),
                  pl.BlockSpec((1, 1, tq, hd), lambda b, h, i: (b, h, i, 0)),
                  pl.BlockSpec((1, 1, tq, hd), lambda b, h, i: (b, h, jnp.minimum(i + 1, nq - 1), 0)),
                  pl.BlockSpec((1, 1, seq, hd), lambda b, h, i: (b, h, 0, 0)),
                  pl.BlockSpec((1, 1, hd, seq), lambda b, h, i: (b, h, 0, 0)),
                  _const_spec(g_col.shape)],
        out_specs=pl.BlockSpec((1, hd, tq), lambda b, h, i: (b, h, i)),
        out_shape=jax.ShapeDtypeStruct((bsz, nh * hd, seq), BF16),
        scratch_shapes=[pltpu.VMEM((rows_aug, seq), BF16), pltpu.VMEM((2, tk, 2 * tq), F32),
                        pltpu.VMEM((1, 2 * tq), F32)],
        compiler_params=_cparams(("parallel", "parallel", "arbitrary")),
        name="diff_attn",
    )(lam2, q, q, k, vt, g_col)


def _gelu_tanh(x):
    return 0.5 * x * (1.0 + jnp.tanh(math.sqrt(2.0 / math.pi) * (x + 0.044715 * (x * x * x))))


def _ffn_kernel(alpha, splits, x_ref, ya_ref, u_ref, yst_ref, yc_ref, ydt_ref,
                wo_ref, sd_ref, wg_ref, bg_ref, g1_ref, b1_ref, w1_ref, w3_ref, w2_ref,
                g2_ref, b2_ref, o_ref):
    gw = ya_ref.shape[-1]
    x = x_ref[...]
    ys = _gelu_tanh(sd_ref[...] * u_ref[...] + yst_ref[0].T)
    gate = jnp.dot(ys.astype(BF16), wg_ref[...], preferred_element_type=F32) + bg_ref[...]
    y_b = (ys * _sigmoid(gate)).astype(BF16)
    mix = jnp.dot(ya_ref[...], wo_ref[0:gw, :], preferred_element_type=F32)
    mix += jnp.dot(y_b, wo_ref[gw:2 * gw, :], preferred_element_type=F32)
    mix += jnp.dot(yc_ref[...], wo_ref[2 * gw:3 * gw, :], preferred_element_type=F32)
    mix += lax.dot_general(ydt_ref[0], wo_ref[3 * gw:4 * gw, :], (((0,), (0,)), ((), ())),
                           preferred_element_type=F32)
    x1 = _layer_norm(alpha * x + mix, g1_ref[...], b1_ref[...])
    x1b = x1.astype(BF16)
    ff = None
    for lo, hi in splits:
        h1 = jnp.dot(x1b, w1_ref[:, lo:hi], preferred_element_type=F32)
        h3 = jnp.dot(x1b, w3_ref[:, lo:hi], preferred_element_type=F32)
        act = (h1 * _sigmoid(h1) * h3).astype(BF16)
        part = jnp.dot(act, w2_ref[lo:hi, :], preferred_element_type=F32)
        ff = part if ff is None else ff + part
    o_ref[...] = _layer_norm(alpha * x1 + ff, g2_ref[...], b2_ref[...])


def _ffn_call(alpha, x2, ya, u, yst, yc, ydt, wo, sd, wg, bgl, g1, b1, w1, w3, w2, g2, b2, bsz, seq):
    t, d = x2.shape
    gw = d // N_MIXERS
    tm = min(FFN_ROWS, seq)
    ns = seq // tm
    hid = w1.shape[-1]
    cut = (hid // 2 + 255) // 256 * 256
    splits = ((0, cut), (cut, hid)) if 0 < cut < hid else ((0, hid),)
    tok = lambda w: pl.BlockSpec((tm, w), lambda i: (i, 0))
    single = lambda a: pl.BlockSpec(a.shape, lambda i: (0,) * a.ndim, pipeline_mode=pl.Buffered(1))
    return pl.pallas_call(
        functools.partial(_ffn_kernel, alpha, splits),
        grid=(t // tm,),
        in_specs=[tok(d), tok(gw), tok(gw),
                  pl.BlockSpec((1, gw, tm), lambda i: (i // ns, 0, i % ns)),
                  tok(gw),
                  pl.BlockSpec((1, gw, tm), lambda i: (i // ns, 0, i % ns)),
                  single(wo), single(sd), single(wg), single(bgl), single(g1), single(b1),
                  single(w1), single(w3), single(w2), single(g2), single(b2)],
        out_specs=tok(d),
        out_shape=jax.ShapeDtypeStruct((t, d), F32),
        compiler_params=_cparams(("parallel",)),
        name="out_ffn",
    )(x2, ya, u, yst, yc, ydt, wo, sd, wg, bgl, g1, b1, w1, w3, w2, g2, b2)


def _rope_tables(seq, gw):
    hd = gw // DA_HEADS
    dq = hd // 2
    rot = dq // 4
    half = rot // 2
    pos = jnp.arange(seq, dtype=F32)
    inv_freq = ROPE_THETA ** (-jnp.arange(0, rot, 2, dtype=F32) / rot)
    ang = pos[:, None] * inv_freq[None, :]
    cos, sin = jnp.cos(ang), jnp.sin(ang)
    lane = jnp.arange(gw) % dq
    idx = lane % half
    lo = (lane < half)[None, :]
    hi = ((lane >= half) & (lane < rot))[None, :]
    cos_l, sin_l = cos[:, idx], sin[:, idx]
    rc = jnp.where(lo | hi, cos_l, 1.0)
    ra = jnp.where(lo, -sin_l, 0.0)
    rb = jnp.where(hi, sin_l, 0.0)
    return rc.astype(F32), ra.astype(F32), rb.astype(F32)


def kernel(x, w_in, w_out, conf_dw_w, conf_dw_b, conf_ln_g, conf_ln_b, s5_a_re, s5_a_im, s5_log_step,
           s5_b_re, s5_b_im, s5_c_re, s5_c_im, s5_d, s5_w_glu, s5_b_glu, sc_conv_w, da_lq1, da_lk1,
           da_lq2, da_lk2, da_subln_g, ln1_g, ln1_b, w_ffn1, w_ffn3, w_ffn2, ln2_g, ln2_b):
    bsz, seq, d = x.shape
    depth = w_in.shape[0]
    gw = d // N_MIXERS
    alpha = (2.0 * depth) ** 0.25
    rc, ra, rb = _rope_tables(seq, gw)
    row = lambda a: a.astype(F32).reshape(1, -1)
    nc = seq // S5_CHUNK
    s5p = _s5_params(s5_a_re, s5_a_im, s5_log_step, s5_b_re, s5_b_im, s5_c_re, s5_c_im, nc)
    x2 = x.reshape(bsz * seq, d)
    for l in range(depth):
        za, u, ut, bg, cv, q, k, vt = _proj_call(x2, w_in[l].astype(BF16), rc, ra, rb, bsz, seq)
        b3 = lambda a: a.reshape(bsz, seq, gw)
        ya, yc = _conv_call(b3(za), b3(cv), b3(bg), conf_dw_w[l].astype(F32), row(conf_dw_b[l]),
                            row(conf_ln_g[l]), row(conf_ln_b[l]), sc_conv_w[l].astype(F32))
        yst = _s5_call(ut.reshape(bsz, gw, nc, S5_CHUNK), *[a[l] for a in s5p]).reshape(bsz, gw, seq)
        lam_init = 0.8 - 0.6 * math.exp(-0.3 * l)
        lam = (jnp.exp(jnp.sum(da_lq1[l].astype(F32) * da_lk1[l].astype(F32)))
               - jnp.exp(jnp.sum(da_lq2[l].astype(F32) * da_lk2[l].astype(F32))) + lam_init)
        lam2 = jnp.stack([lam, jnp.asarray(1.0 - lam_init, F32)]).astype(F32)
        ydt = _attn_call(lam2, q, k, vt, da_subln_g[l].astype(F32).reshape(-1, 1))
        x2 = _ffn_call(alpha, x2, ya.reshape(-1, gw), u, yst, yc.reshape(-1, gw), ydt,
                       w_out[l].astype(BF16), row(s5_d[l]), s5_w_glu[l].astype(BF16), row(s5_b_glu[l]),
                       row(ln1_g[l]), row(ln1_b[l]), w_ffn1[l].astype(BF16), w_ffn3[l].astype(BF16),
                       w_ffn2[l].astype(BF16), row(ln2_g[l]), row(ln2_b[l]), bsz, seq)
    return x2.reshape(bsz, seq, d)
```

```python
import functools
import math

import jax
import jax.numpy as jnp
from jax import lax
from jax.experimental import pallas as pl
from jax.experimental.pallas import tpu as pltpu

F32 = jnp.float32
BF16 = jnp.bfloat16

N_MIXERS = 4
CONV_K = 31
S5_CH = 16
S5_STATE = 64
SHORT_K = 3
DA_HEADS = 4
ROPE_THETA = 500000.0
LN_EPS = 1e-5

SUBLANES = 8
LANES = 128
VMEM_LIMIT_BYTES = 56 * 1024 * 1024

PROJ_ROWS = 512
CONV_ROWS = 512
CONV_HALO = 16
CONV_CHUNK = 128
S5_CHUNK = LANES
ATT_Q = 256
ATT_K = 1024
FFN_ROWS = 512


def _cparams(sem):
    return pltpu.CompilerParams(dimension_semantics=sem, vmem_limit_bytes=VMEM_LIMIT_BYTES)


def _const_spec(shape):
    nd = len(shape)
    return pl.BlockSpec(shape, lambda *_: (0,) * nd)


def _sigmoid(x):
    return 1.0 / (1.0 + jnp.exp(-x))


def _layer_norm(x, g, b):
    mu = jnp.mean(x, axis=-1, keepdims=True)
    xc = x - mu
    var = jnp.mean(xc * xc, axis=-1, keepdims=True)
    return xc * lax.rsqrt(var + LN_EPS) * g + b


def _proj_kernel(gw, x_ref, w_ref, rc_ref, ra_ref, rb_ref,
                 za_ref, u_ref, ut_ref, bg_ref, cv_ref, q_ref, k_ref, vt_ref):
    xb = x_ref[...].astype(BF16)

    def seg(i):
        return jnp.dot(xb, w_ref[:, i * gw:(i + 1) * gw], preferred_element_type=F32)

    za_ref[...] = seg(0) * _sigmoid(seg(1))
    u = seg(2)
    u_ref[...] = u
    ut_ref[0] = u.T.astype(BF16)
    bg_ref[...] = seg(3)
    cv_ref[...] = seg(4) * seg(5)

    rc, ra, rb = rc_ref[...], ra_ref[...], rb_ref[...]
    hd = gw // DA_HEADS
    half = hd // 2 // 4 // 2

    def rope(t):
        return t * rc + pltpu.roll(t, gw - half, axis=1) * ra + pltpu.roll(t, half, axis=1) * rb

    scale = (hd // 2) ** -0.5 * math.log2(math.e)
    q = rope(seg(6)) * scale
    k = rope(seg(7))
    vt = seg(8).T
    for h in range(DA_HEADS):
        q_ref[0, h] = q[:, h * hd:(h + 1) * hd].astype(BF16)
        k_ref[0, h] = k[:, h * hd:(h + 1) * hd].astype(BF16)
        vt_ref[0, h] = vt[h * hd:(h + 1) * hd, :].astype(BF16)


def _proj_call(x2, w_in, rc, ra, rb, bsz, seq):
    t, d = x2.shape
    gw = d // N_MIXERS
    tm = min(PROJ_ROWS, seq)
    ns = seq // tm
    hd = gw // DA_HEADS
    tok = lambda: pl.BlockSpec((tm, gw), lambda i: (i, 0))
    rope_spec = lambda: pl.BlockSpec((tm, gw), lambda i: (i % ns, 0))
    out_shape = (
        [jax.ShapeDtypeStruct((t, gw), F32)] * 2
        + [jax.ShapeDtypeStruct((bsz, gw, seq), BF16)]
        + [jax.ShapeDtypeStruct((t, gw), F32)] * 2
        + [jax.ShapeDtypeStruct((bsz, DA_HEADS, seq, hd), BF16)] * 2
        + [jax.ShapeDtypeStruct((bsz, DA_HEADS, hd, seq), BF16)]
    )
    return pl.pallas_call(
        functools.partial(_proj_kernel, gw),
        grid=(t // tm,),
        in_specs=[pl.BlockSpec((tm, d), lambda i: (i, 0)), _const_spec(w_in.shape),
                  rope_spec(), rope_spec(), rope_spec()],
        out_specs=[tok(), tok(), pl.BlockSpec((1, gw, tm), lambda i: (i // ns, 0, i % ns)), tok(), tok(),
                   pl.BlockSpec((1, DA_HEADS, tm, hd), lambda i: (i // ns, 0, i % ns, 0)),
                   pl.BlockSpec((1, DA_HEADS, tm, hd), lambda i: (i // ns, 0, i % ns, 0)),
                   pl.BlockSpec((1, DA_HEADS, hd, tm), lambda i: (i // ns, 0, 0, i % ns))],
        out_shape=out_shape,
        compiler_params=_cparams(("parallel",)),
        name="proj_in",
    )(x2, w_in, rc, ra, rb)


def _conv_kernel(ts, za_ref, zp_ref, zn_ref, cv_ref, cp_ref, cn_ref, bg_ref,
                 dw_ref, db_ref, g_ref, b_ref, sw_ref, ya_ref, yc_ref, zbuf, cbuf):
    j = pl.program_id(1)
    nj = pl.num_programs(1)
    keep_prev = jnp.where(j > 0, 1.0, 0.0).astype(F32)
    keep_next = jnp.where(j < nj - 1, 1.0, 0.0).astype(F32)
    h = CONV_HALO
    zbuf[0:h, :] = zp_ref[0] * keep_prev
    zbuf[h:h + ts, :] = za_ref[0]
    zbuf[h + ts:2 * h + ts, :] = zn_ref[0] * keep_next
    cbuf[0:h, :] = cp_ref[0] * keep_prev
    cbuf[h:h + ts, :] = cv_ref[0]
    cbuf[h + ts:2 * h + ts, :] = cn_ref[0] * keep_next

    rows = CONV_CHUNK
    win_rows = rows + 2 * h

    def taps(buf, base, w_ref, ntaps):
        win = buf[pl.ds(base, win_rows), :]
        first = h - ntaps // 2
        acc = jnp.zeros((rows, win.shape[-1]), F32)
        for phase in range(SUBLANES):
            offs = [o for o in range(first, first + ntaps) if o % SUBLANES == phase]
            if not offs:
                continue
            rolled = win if phase == 0 else pltpu.roll(win, win_rows - phase, axis=0)
            for o in offs:
                lo = o - phase
                acc = acc + w_ref[o - first:o - first + 1, :] * rolled[lo:lo + rows, :]
        return acc

    def chunk(c, carry):
        base = pl.multiple_of(c * rows, rows)
        acc = taps(zbuf, base, dw_ref, CONV_K)
        z = _layer_norm(acc + db_ref[...], g_ref[...], b_ref[...])
        ya_ref[0, pl.ds(base, rows), :] = (z * _sigmoid(z)).astype(ya_ref.dtype)
        sc = taps(cbuf, base, sw_ref, SHORT_K)
        yc_ref[0, pl.ds(base, rows), :] = (bg_ref[0, pl.ds(base, rows), :] * sc).astype(yc_ref.dtype)
        return carry

    lax.fori_loop(0, ts // rows, chunk, 0)


def _conv_call(za, cv, bg, dw_w, dw_b, ln_g, ln_b, sc_w):
    bsz, seq, gw = za.shape
    ts = min(CONV_ROWS, seq)
    r = ts // CONV_HALO
    nh = seq // CONV_HALO
    main = lambda: pl.BlockSpec((1, ts, gw), lambda b, j: (b, j, 0))
    prev = lambda: pl.BlockSpec((1, CONV_HALO, gw), lambda b, j: (b, jnp.maximum(j * r - 1, 0), 0))
    nxt = lambda: pl.BlockSpec((1, CONV_HALO, gw), lambda b, j: (b, jnp.minimum((j + 1) * r, nh - 1), 0))
    return pl.pallas_call(
        functools.partial(_conv_kernel, ts),
        grid=(bsz, seq // ts),
        in_specs=[main(), prev(), nxt(), main(), prev(), nxt(), main(),
                  _const_spec(dw_w.shape), _const_spec(dw_b.shape), _const_spec(ln_g.shape),
                  _const_spec(ln_b.shape), _const_spec(sc_w.shape)],
        out_specs=[main(), main()],
        out_shape=[jax.ShapeDtypeStruct((bsz, seq, gw), BF16)] * 2,
        scratch_shapes=[pltpu.VMEM((ts + 2 * CONV_HALO, gw), F32)] * 2,
        compiler_params=_cparams(("parallel", "parallel")),
        name="dwconv",
    )(za, za, za, cv, cv, cv, bg, dw_w, dw_b, ln_g, ln_b, sc_w)


def _s5_kernel(nc, ut_ref, kv_ref, bc_ref, cc_ref, p1_ref, p2_ref, y_ref, t_ref):
    bsz, pch, _, lc = ut_ref.shape
    rows = bsz * nc
    nst = bc_ref.shape[-1] // 4

    def build(q, carry):
        kvq = kv_ref[0, pl.ds(pl.multiple_of(q * pch, pch), pch), :]
        for p in range(pch):
            x = jnp.broadcast_to(kvq[p:p + 1, :], (lc, 2 * lc))
            tile = pltpu.roll(x, 0, axis=1, stride=1, stride_axis=0)[:, lc:]
            t_ref[pl.ds(pl.multiple_of(q * lc, lc), lc), p * lc:(p + 1) * lc] = tile.astype(t_ref.dtype)
        return carry

    lax.fori_loop(0, pch, build, 0)

    v = jnp.concatenate([ut_ref[:, q].reshape(rows, lc) for q in range(pch)], axis=1)
    y = jnp.dot(v, t_ref[...], preferred_element_type=F32)
    w = jnp.dot(v, bc_ref[0], preferred_element_type=F32)

    cidx = lax.broadcasted_iota(jnp.int32, (bsz, nc, 2 * nst), 1).reshape(rows, 2 * nst)

    def cmul(x, lvl, half):
        lo, hi = half * 2 * nst, (half + 1) * 2 * nst
        return x * p1_ref[0, lvl:lvl + 1, lo:hi] + pltpu.roll(x, nst, axis=1) * p2_ref[0, lvl:lvl + 1, lo:hi]

    def chunk_scan(x, half, reverse):
        lvl, k = 0, 1
        while k < nc:
            sh = pltpu.roll(x, (rows - k) if reverse else k, axis=0)
            keep = (cidx <= nc - 1 - k) if reverse else (cidx >= k)
            x = x + cmul(jnp.where(keep, sh, 0.0), lvl, half)
            lvl, k = lvl + 1, 2 * k
        sh = pltpu.roll(x, (rows - 1) if reverse else 1, axis=0)
        keep = (cidx <= nc - 2) if reverse else (cidx >= 1)
        return jnp.where(keep, sh, 0.0)

    xprev = jnp.concatenate([chunk_scan(w[:, 0:2 * nst], 0, False),
                             chunk_scan(w[:, 2 * nst:4 * nst], 1, True)], axis=1)
    cc = jnp.concatenate([cc_ref[0, p] for p in range(pch)], axis=1)
    y = y + jnp.dot(xprev.astype(BF16), cc, preferred_element_type=F32)
    for p in range(pch):
        y_ref[:, p] = y[:, p * lc:(p + 1) * lc].reshape(bsz, nc, lc)


def _s5_call(ut4, kv, bc, cc, p1, p2):
    bsz, gw, nc, lc = ut4.shape
    ng = kv.shape[0]
    pch = gw // ng
    gspec = lambda a: pl.BlockSpec((1,) + a.shape[1:], lambda g: (g,) + (0,) * (a.ndim - 1))
    io = lambda: pl.BlockSpec((bsz, pch, nc, lc), lambda g: (0, g, 0, 0))
    return pl.pallas_call(
        functools.partial(_s5_kernel, nc),
        grid=(ng,),
        in_specs=[io(), gspec(kv), gspec(bc), gspec(cc), gspec(p1), gspec(p2)],
        out_specs=io(),
        out_shape=jax.ShapeDtypeStruct((bsz, gw, nc, lc), F32),
        scratch_shapes=[pltpu.VMEM((pch * lc, pch * lc), BF16)],
        compiler_params=_cparams(("parallel",)),
        name="s5_mix",
    )(ut4, kv, bc, cc, p1, p2)


def _s5_params(a_re, a_im, log_step, b_re, b_im, c_re, c_im, nc):
    lc = S5_CHUNK
    hi = lax.Precision.HIGHEST
    step = jnp.exp(log_step.astype(F32))[..., None]
    lr, li = a_re.astype(F32), a_im.astype(F32)
    mag = jnp.exp(lr * step)
    abr, abi = mag * jnp.cos(li * step), mag * jnp.sin(li * step)
    den = lr * lr + li * li
    pr = abr - 1.0
    fr = (pr * lr + abi * li) / den
    fi = (abi * lr - pr * li) / den
    br, bi = b_re.astype(F32), b_im.astype(F32)
    bbr = fr[..., None] * br - fi[..., None] * bi
    bbi = fr[..., None] * bi + fi[..., None] * br
    cr, ci = c_re.astype(F32), c_im.astype(F32)
    ls, an = lr * step, li * step
    nly, _, ng, nst = ls.shape
    pch = br.shape[-1]

    def apow(ks):
        m = jnp.exp(ls[..., None] * ks)
        return m * jnp.cos(an[..., None] * ks), m * jnp.sin(an[..., None] * ks)

    mr, mi = apow(jnp.arange(lc + 1, dtype=F32))
    mrt, mit = jnp.swapaxes(mr, -1, -2), jnp.swapaxes(mi, -1, -2)

    bt_r, bt_i = jnp.swapaxes(bbr, -1, -2), jnp.swapaxes(bbi, -1, -2)

    def lag_kernels(d):
        cb_r = bt_r[:, d, :, :, None, :] * cr[:, d, :, None] - bt_i[:, d, :, :, None, :] * ci[:, d, :, None]
        cb_i = bt_r[:, d, :, :, None, :] * ci[:, d, :, None] + bt_i[:, d, :, :, None, :] * cr[:, d, :, None]
        cb_r = cb_r.reshape(nly, ng, pch * pch, nst)
        cb_i = cb_i.reshape(nly, ng, pch * pch, nst)
        return (jnp.einsum('lgxn,lgnk->lgxk', cb_r, mr[:, d, :, :, :lc], precision=hi)
                - jnp.einsum('lgxn,lgnk->lgxk', cb_i, mi[:, d, :, :, :lc], precision=hi))

    kf, kb = lag_kernels(0), lag_kernels(1)
    left = jnp.concatenate([jnp.zeros_like(kb[..., :1]), jnp.flip(kb[..., 1:], axis=-1)], axis=-1)
    right = kf.at[..., 0].add(kb[..., 0])
    kv = jnp.concatenate([left, right], axis=-1)

    f_r, f_i = jnp.flip(mrt[:, 0, :, :lc], axis=2), jnp.flip(mit[:, 0, :, :lc], axis=2)
    b_r, b_i = mrt[:, 1, :, :lc], mit[:, 1, :, :lc]
    pa = jnp.concatenate([f_r, f_i, b_r, b_i], axis=-1)[:, :, None]
    pb = jnp.concatenate([-f_i, f_r, -b_i, b_r], axis=-1)[:, :, None]
    bq1 = jnp.concatenate([bt_r[:, 0], bt_r[:, 0], bt_r[:, 1], bt_r[:, 1]], axis=-1)[:, :, :, None]
    bq2 = jnp.concatenate([bt_i[:, 0], bt_i[:, 0], bt_i[:, 1], bt_i[:, 1]], axis=-1)[:, :, :, None]
    bc = (pa * bq1 + pb * bq2).reshape(nly, ng, pch * lc, 4 * nst).astype(BF16)

    qf_r, qf_i = mr[:, 0, :, :, 1:lc + 1], mi[:, 0, :, :, 1:lc + 1]
    qb_r, qb_i = jnp.flip(mr[:, 1, :, :, 1:lc + 1], axis=-1), jnp.flip(mi[:, 1, :, :, 1:lc + 1], axis=-1)
    qa = jnp.concatenate([qf_r, -qf_i, qb_r, -qb_i], axis=2)[:, :, None]
    qb = jnp.concatenate([-qf_i, -qf_r, -qb_i, -qb_r], axis=2)[:, :, None]
    c1 = jnp.concatenate([cr[:, 0], cr[:, 0], cr[:, 1], cr[:, 1]], axis=-1)[..., None]
    c2 = jnp.concatenate([ci[:, 0], ci[:, 0], ci[:, 1], ci[:, 1]], axis=-1)[..., None]
    cc = (qa * c1 + qb * c2).astype(BF16)

    nlev = max(1, (nc - 1).bit_length())
    lv_r, lv_i = apow(float(lc) * 2.0 ** jnp.arange(nlev, dtype=F32))
    lv_r, lv_i = jnp.swapaxes(lv_r, -1, -2), jnp.swapaxes(lv_i, -1, -2)
    p1 = jnp.concatenate([lv_r[:, 0], lv_r[:, 0], lv_r[:, 1], lv_r[:, 1]], axis=-1)
    p2 = jnp.concatenate([-lv_i[:, 0], lv_i[:, 0], -lv_i[:, 1], lv_i[:, 1]], axis=-1)
    return kv, bc, cc, p1, p2


def _attn_kernel(tk, lam_ref, q_ref, qn_ref, k_ref, vt_ref, g_ref, o_ref, va_ref, s_ref, mb_ref):
    hd = q_ref.shape[-1]
    dq = hd // 2
    tq = q_ref.shape[2]
    seq = k_ref.shape[2]
    rows_aug = va_ref.shape[0]

    def split_maps(q):
        lane = lax.broadcasted_iota(jnp.int32, q.shape, 1)
        zero = jnp.zeros_like(q)
        return jnp.concatenate([jnp.where(lane < dq, q, zero), jnp.where(lane >= dq, q, zero)], axis=0)

    nblk = seq // tk
    kc = min(tk, 2 * LANES)

    def scores(qq, i, slot):
        kb = k_ref[0, 0, pl.ds(pl.multiple_of(i * tk, tk), tk), :]
        s = lax.dot_general(kb, qq, (((1,), (1,)), ((), ())), preferred_element_type=F32)
        s_ref[slot] = s
        return jnp.max(s, axis=0, keepdims=True)

    qq = split_maps(q_ref[0, 0])

    @pl.when(pl.program_id(2) == 0)
    def _():
        va_ref[0:hd, :] = vt_ref[0, 0]
        row = lax.broadcasted_iota(jnp.int32, (rows_aug - hd, seq), 0)
        va_ref[hd:rows_aug, :] = jnp.where(row == 0, 1.0, 0.0).astype(va_ref.dtype)
        mb_ref[...] = scores(qq, 0, 0)

    def softmax_pv(i, slot, m, mb, acc1, acc2):
        m_new = jnp.maximum(m, mb)
        alpha = jnp.exp2(m - m_new)
        d1 = None
        d2 = None
        for c in range(tk // kc):
            p = jnp.exp2(s_ref[slot, c * kc:(c + 1) * kc, :] - m_new).astype(BF16)
            vb = va_ref[:, pl.ds(pl.multiple_of(i * tk + c * kc, kc), kc)]
            t1 = jnp.dot(vb, p[:, 0:tq], preferred_element_type=F32)
            t2 = jnp.dot(vb, p[:, tq:2 * tq], preferred_element_type=F32)
            d1 = t1 if d1 is None else d1 + t1
            d2 = t2 if d2 is None else d2 + t2
        return m_new, acc1 * alpha[:, 0:tq] + d1, acc2 * alpha[:, tq:2 * tq] + d2

    m = jnp.full((1, 2 * tq), -jnp.inf, F32)
    acc1 = jnp.zeros((rows_aug, tq), F32)
    acc2 = acc1
    assert nblk % 2 == 0
    mb = mb_ref[...]

    def pair(j, carry):
        m, mb, acc1, acc2 = carry
        mb1 = scores(qq, 2 * j + 1, 1)
        m, acc1, acc2 = softmax_pv(2 * j, 0, m, mb, acc1, acc2)
        mb2 = scores(qq, 2 * j + 2, 0)
        m, acc1, acc2 = softmax_pv(2 * j + 1, 1, m, mb1, acc1, acc2)
        return m, mb2, acc1, acc2

    m, mb, acc1, acc2 = lax.fori_loop(0, nblk // 2 - 1, pair, (m, mb, acc1, acc2))
    mb1 = scores(qq, nblk - 1, 1)
    m, acc1, acc2 = softmax_pv(nblk - 2, 0, m, mb, acc1, acc2)
    mb_ref[...] = scores(split_maps(qn_ref[0, 0]), 0, 0)
    m, acc1, acc2 = softmax_pv(nblk - 1, 1, m, mb1, acc1, acc2)
    lam = lam_ref[0]
    post = lam_ref[1]
    o = acc1[0:hd, :] / acc1[hd:hd + 1, :] - lam * (acc2[0:hd, :] / acc2[hd:hd + 1, :])
    o = o * lax.rsqrt(jnp.mean(o * o, axis=0, keepdims=True) + LN_EPS)
    o_ref[0] = (o * g_ref[...] * post).astype(o_ref.dtype)


def _attn_call(lam2, q, k, vt, g_col):
    bsz, nh, seq, hd = q.shape
    tq = min(ATT_Q, seq)
    tk = min(ATT_K, seq)
    rows_aug = hd + 16
    nq = seq // tq
    return pl.pallas_call(
        functools.partial(_attn_kernel, tk),
        grid=(bsz, nh, nq),
        in_specs=[pl.BlockSpec(memory_space=pltpu.SMEM),
                  pl.BlockSpec((1, 1, tq, hd), lambda b, h, i: (b, h, i, 0)),
                  pl.BlockSpec((1, 1, tq, hd), lambda b, h, i: (b, h, jnp.minimum(i + 1, nq - 1), 0)),
                  pl.BlockSpec((1, 1, seq, hd), lambda b, h, i: (b, h, 0, 0)),
                  pl.BlockSpec((1, 1, hd, seq), lambda b, h, i: (b, h, 0, 0)),
                  _const_spec(g_col.shape)],
        out_specs=pl.BlockSpec((1, hd, tq), lambda b, h, i: (b, h, i)),
        out_shape=jax.ShapeDtypeStruct((bsz, nh * hd, seq), BF16),
        scratch_shapes=[pltpu.VMEM((rows_aug, seq), BF16), pltpu.VMEM((2, tk, 2 * tq), F32),
                        pltpu.VMEM((1, 2 * tq), F32)],
        compiler_params=_cparams(("parallel", "parallel", "arbitrary")),
        name="diff_attn",
    )(lam2, q, q, k, vt, g_col)


def _gelu_tanh(x):
    return 0.5 * x * (1.0 + jnp.tanh(math.sqrt(2.0 / math.pi) * (x + 0.044715 * (x * x * x))))


def _ffn_kernel(alpha, splits, x_ref, ya_ref, u_ref, yst_ref, yc_ref, ydt_ref,
                wo_ref, sd_ref, wg_ref, bg_ref, g1_ref, b1_ref, w1_ref, w3_ref, w2_ref,
                g2_ref, b2_ref, o_ref):
    gw = ya_ref.shape[-1]
    x = x_ref[...]
    ys = _gelu_tanh(sd_ref[...] * u_ref[...] + yst_ref[0].T)
    gate = jnp.dot(ys.astype(BF16), wg_ref[...], preferred_element_type=F32) + bg_ref[...]
    y_b = (ys * _sigmoid(gate)).astype(BF16)
    mix = jnp.dot(ya_ref[...], wo_ref[0:gw, :], preferred_element_type=F32)
    mix += jnp.dot(y_b, wo_ref[gw:2 * gw, :], preferred_element_type=F32)
    mix += jnp.dot(yc_ref[...], wo_ref[2 * gw:3 * gw, :], preferred_element_type=F32)
    mix += lax.dot_general(ydt_ref[0], wo_ref[3 * gw:4 * gw, :], (((0,), (0,)), ((), ())),
                           preferred_element_type=F32)
    x1 = _layer_norm(alpha * x + mix, g1_ref[...], b1_ref[...])
    x1b = x1.astype(BF16)
    ff = None
    for lo, hi in splits:
        h1 = jnp.dot(x1b, w1_ref[:, lo:hi], preferred_element_type=F32)
        h3 = jnp.dot(x1b, w3_ref[:, lo:hi], preferred_element_type=F32)
        act = (h1 * _sigmoid(h1) * h3).astype(BF16)
        part = jnp.dot(act, w2_ref[lo:hi, :], preferred_element_type=F32)
        ff = part if ff is None else ff + part
    o_ref[...] = _layer_norm(alpha * x1 + ff, g2_ref[...], b2_ref[...])


def _ffn_call(alpha, x2, ya, u, yst, yc, ydt, wo, sd, wg, bgl, g1, b1, w1, w3, w2, g2, b2, bsz, seq):
    t, d = x2.shape
    gw = d // N_MIXERS
    tm = min(FFN_ROWS, seq)
    ns = seq // tm
    hid = w1.shape[-1]
    cut = (hid // 2 + 255) // 256 * 256
    splits = ((0, cut), (cut, hid)) if 0 < cut < hid else ((0, hid),)
    tok = lambda w: pl.BlockSpec((tm, w), lambda i: (i, 0))
    single = lambda a: pl.BlockSpec(a.shape, lambda i: (0,) * a.ndim, pipeline_mode=pl.Buffered(1))
    return pl.pallas_call(
        functools.partial(_ffn_kernel, alpha, splits),
        grid=(t // tm,),
        in_specs=[tok(d), tok(gw), tok(gw),
                  pl.BlockSpec((1, gw, tm), lambda i: (i // ns, 0, i % ns)),
                  tok(gw),
                  pl.BlockSpec((1, gw, tm), lambda i: (i // ns, 0, i % ns)),
                  single(wo), single(sd), single(wg), single(bgl), single(g1), single(b1),
                  single(w1), single(w3), single(w2), single(g2), single(b2)],
        out_specs=tok(d),
        out_shape=jax.ShapeDtypeStruct((t, d), F32),
        compiler_params=_cparams(("parallel",)),
        name="out_ffn",
    )(x2, ya, u, yst, yc, ydt, wo, sd, wg, bgl, g1, b1, w1, w3, w2, g2, b2)


def _rope_tables(seq, gw):
    hd = gw // DA_HEADS
    dq = hd // 2
    rot = dq // 4
    half = rot // 2
    pos = jnp.arange(seq, dtype=F32)
    inv_freq = ROPE_THETA ** (-jnp.arange(0, rot, 2, dtype=F32) / rot)
    ang = pos[:, None] * inv_freq[None, :]
    cos, sin = jnp.cos(ang), jnp.sin(ang)
    lane = jnp.arange(gw) % dq
    idx = lane % half
    lo = (lane < half)[None, :]
    hi = ((lane >= half) & (lane < rot))[None, :]
    cos_l, sin_l = cos[:, idx], sin[:, idx]
    rc = jnp.where(lo | hi, cos_l, 1.0)
    ra = jnp.where(lo, -sin_l, 0.0)
    rb = jnp.where(hi, sin_l, 0.0)
    return rc.astype(F32), ra.astype(F32), rb.astype(F32)


def kernel(x, w_in, w_out, conf_dw_w, conf_dw_b, conf_ln_g, conf_ln_b, s5_a_re, s5_a_im, s5_log_step,
           s5_b_re, s5_b_im, s5_c_re, s5_c_im, s5_d, s5_w_glu, s5_b_glu, sc_conv_w, da_lq1, da_lk1,
           da_lq2, da_lk2, da_subln_g, ln1_g, ln1_b, w_ffn1, w_ffn3, w_ffn2, ln2_g, ln2_b):
    bsz, seq, d = x.shape
    depth = w_in.shape[0]
    gw = d // N_MIXERS
    alpha = (2.0 * depth) ** 0.25
    rc, ra, rb = _rope_tables(seq, gw)
    row = lambda a: a.astype(F32).reshape(1, -1)
    nc = seq // S5_CHUNK
    s5p = _s5_params(s5_a_re, s5_a_im, s5_log_step, s5_b_re, s5_b_im, s5_c_re, s5_c_im, nc)
    x2 = x.reshape(bsz * seq, d)
    for l in range(depth):
        za, u, ut, bg, cv, q, k, vt = _proj_call(x2, w_in[l].astype(BF16), rc, ra, rb, bsz, seq)
        b3 = lambda a: a.reshape(bsz, seq, gw)
        ya, yc = _conv_call(b3(za), b3(cv), b3(bg), conf_dw_w[l].astype(F32), row(conf_dw_b[l]),
                            row(conf_ln_g[l]), row(conf_ln_b[l]), sc_conv_w[l].astype(F32))
        yst = _s5_call(ut.reshape(bsz, gw, nc, S5_CHUNK), *[a[l] for a in s5p]).reshape(bsz, gw, seq)
        lam_init = 0.8 - 0.6 * math.exp(-0.3 * l)
        lam = (jnp.exp(jnp.sum(da_lq1[l].astype(F32) * da_lk1[l].astype(F32)))
               - jnp.exp(jnp.sum(da_lq2[l].astype(F32) * da_lk2[l].astype(F32))) + lam_init)
        lam2 = jnp.stack([lam, jnp.asarray(1.0 - lam_init, F32)]).astype(F32)
        ydt = _attn_call(lam2, q, k, vt, da_subln_g[l].astype(F32).reshape(-1, 1))
        x2 = _ffn_call(alpha, x2, ya.reshape(-1, gw), u, yst, yc.reshape(-1, gw), ydt,
                       w_out[l].astype(BF16), row(s5_d[l]), s5_w_glu[l].astype(BF16), row(s5_b_glu[l]),
                       row(ln1_g[l]), row(ln1_b[l]), w_ffn1[l].astype(BF16), w_ffn3[l].astype(BF16),
                       w_ffn2[l].astype(BF16), row(ln2_g[l]), row(ln2_b[l]), bsz, seq)
    return x2.reshape(bsz, seq, d)
```

```python
import functools
import math

import jax
import jax.numpy as jnp
from jax import lax
from jax.experimental import pallas as pl
from jax.experimental.pallas import tpu as pltpu

F32 = jnp.float32
BF16 = jnp.bfloat16

N_MIXERS = 4
CONV_K = 31
S5_CH = 16
S5_STATE = 64
SHORT_K = 3
DA_HEADS = 4
ROPE_THETA = 500000.0
LN_EPS = 1e-5

SUBLANES = 8
LANES = 128
VMEM_LIMIT_BYTES = 56 * 1024 * 1024

PROJ_ROWS = 512
CONV_ROWS = 512
CONV_HALO = 16
CONV_CHUNK = 128
S5_CHUNK = LANES
ATT_Q = 256
ATT_K = 1024
FFN_ROWS = 512


def _cparams(sem):
    return pltpu.CompilerParams(dimension_semantics=sem, vmem_limit_bytes=VMEM_LIMIT_BYTES)


def _const_spec(shape):
    nd = len(shape)
    return pl.BlockSpec(shape, lambda *_: (0,) * nd)


def _layer_spec(a, layer, **kw):
    return pl.BlockSpec((None,) + a.shape[1:], lambda *_: (layer,) + (0,) * (a.ndim - 1), **kw)


def _sigmoid(x):
    return 1.0 / (1.0 + jnp.exp(-x))


def _layer_norm(x, g, b):
    mu = jnp.mean(x, axis=-1, keepdims=True)
    xc = x - mu
    var = jnp.mean(xc * xc, axis=-1, keepdims=True)
    return xc * lax.rsqrt(var + LN_EPS) * g + b


def _proj_kernel(gw, x_ref, w_ref, rc_ref, ra_ref, rb_ref,
                 za_ref, u_ref, ut_ref, bg_ref, cv_ref, q_ref, k_ref, vt_ref):
    xb = x_ref[...].astype(BF16)

    def seg(i):
        return jnp.dot(xb, w_ref[:, i * gw:(i + 1) * gw], preferred_element_type=F32)

    za_ref[...] = seg(0) * _sigmoid(seg(1))
    u = seg(2)
    u_ref[...] = u
    ut_ref[0] = u.T.astype(BF16)
    bg_ref[...] = seg(3)
    cv_ref[...] = seg(4) * seg(5)

    rc, ra, rb = rc_ref[...], ra_ref[...], rb_ref[...]
    hd = gw // DA_HEADS
    half = hd // 2 // 4 // 2

    def rope(t):
        return t * rc + pltpu.roll(t, gw - half, axis=1) * ra + pltpu.roll(t, half, axis=1) * rb

    scale = (hd // 2) ** -0.5 * math.log2(math.e)
    q = rope(seg(6)) * scale
    k = rope(seg(7))
    vt = seg(8).T
    for h in range(DA_HEADS):
        q_ref[0, h] = q[:, h * hd:(h + 1) * hd].astype(BF16)
        k_ref[0, h] = k[:, h * hd:(h + 1) * hd].astype(BF16)
        vt_ref[0, h] = vt[h * hd:(h + 1) * hd, :].astype(BF16)


def _proj_call(layer, x2, w_in, rc, ra, rb, bsz, seq):
    t, d = x2.shape
    gw = d // N_MIXERS
    tm = min(PROJ_ROWS, seq)
    ns = seq // tm
    hd = gw // DA_HEADS
    tok = lambda: pl.BlockSpec((tm, gw), lambda i: (i, 0))
    rope_spec = lambda: pl.BlockSpec((tm, gw), lambda i: (i % ns, 0))
    out_shape = (
        [jax.ShapeDtypeStruct((t, gw), F32)] * 2
        + [jax.ShapeDtypeStruct((bsz, gw, seq), BF16)]
        + [jax.ShapeDtypeStruct((t, gw), F32)] * 2
        + [jax.ShapeDtypeStruct((bsz, DA_HEADS, seq, hd), BF16)] * 2
        + [jax.ShapeDtypeStruct((bsz, DA_HEADS, hd, seq), BF16)]
    )
    return pl.pallas_call(
        functools.partial(_proj_kernel, gw),
        grid=(t // tm,),
        in_specs=[pl.BlockSpec((tm, d), lambda i: (i, 0)), _layer_spec(w_in, layer),
                  rope_spec(), rope_spec(), rope_spec()],
        out_specs=[tok(), tok(), pl.BlockSpec((1, gw, tm), lambda i: (i // ns, 0, i % ns)), tok(), tok(),
                   pl.BlockSpec((1, DA_HEADS, tm, hd), lambda i: (i // ns, 0, i % ns, 0)),
                   pl.BlockSpec((1, DA_HEADS, tm, hd), lambda i: (i // ns, 0, i % ns, 0)),
                   pl.BlockSpec((1, DA_HEADS, hd, tm), lambda i: (i // ns, 0, 0, i % ns))],
        out_shape=out_shape,
        compiler_params=_cparams(("parallel",)),
        name="proj_in",
    )(x2, w_in, rc, ra, rb)


def _conv_kernel(ts, za_ref, zp_ref, zn_ref, cv_ref, cp_ref, cn_ref, bg_ref,
                 dw_ref, db_ref, g_ref, b_ref, sw_ref, ya_ref, yc_ref, zbuf, cbuf):
    j = pl.program_id(1)
    nj = pl.num_programs(1)
    keep_prev = jnp.where(j > 0, 1.0, 0.0).astype(F32)
    keep_next = jnp.where(j < nj - 1, 1.0, 0.0).astype(F32)
    h = CONV_HALO
    zbuf[0:h, :] = zp_ref[0] * keep_prev
    zbuf[h:h + ts, :] = za_ref[0]
    zbuf[h + ts:2 * h + ts, :] = zn_ref[0] * keep_next
    cbuf[0:h, :] = cp_ref[0] * keep_prev
    cbuf[h:h + ts, :] = cv_ref[0]
    cbuf[h + ts:2 * h + ts, :] = cn_ref[0] * keep_next

    rows = CONV_CHUNK
    win_rows = rows + 2 * h

    def taps(buf, base, w_ref, ntaps):
        win = buf[pl.ds(base, win_rows), :]
        first = h - ntaps // 2
        acc = jnp.zeros((rows, win.shape[-1]), F32)
        for phase in range(SUBLANES):
            offs = [o for o in range(first, first + ntaps) if o % SUBLANES == phase]
            if not offs:
                continue
            rolled = win if phase == 0 else pltpu.roll(win, win_rows - phase, axis=0)
            for o in offs:
                lo = o - phase
                acc = acc + w_ref[o - first:o - first + 1, :] * rolled[lo:lo + rows, :]
        return acc

    def chunk(c, carry):
        base = pl.multiple_of(c * rows, rows)
        acc = taps(zbuf, base, dw_ref, CONV_K)
        z = _layer_norm(acc + db_ref[...], g_ref[...], b_ref[...])
        ya_ref[0, pl.ds(base, rows), :] = (z * _sigmoid(z)).astype(ya_ref.dtype)
        sc = taps(cbuf, base, sw_ref, SHORT_K)
        yc_ref[0, pl.ds(base, rows), :] = (bg_ref[0, pl.ds(base, rows), :] * sc).astype(yc_ref.dtype)
        return carry

    lax.fori_loop(0, ts // rows, chunk, 0)


def _conv_call(za, cv, bg, dw_w, dw_b, ln_g, ln_b, sc_w):
    bsz, seq, gw = za.shape
    ts = min(CONV_ROWS, seq)
    r = ts // CONV_HALO
    nh = seq // CONV_HALO
    main = lambda: pl.BlockSpec((1, ts, gw), lambda b, j: (b, j, 0))
    prev = lambda: pl.BlockSpec((1, CONV_HALO, gw), lambda b, j: (b, jnp.maximum(j * r - 1, 0), 0))
    nxt = lambda: pl.BlockSpec((1, CONV_HALO, gw), lambda b, j: (b, jnp.minimum((j + 1) * r, nh - 1), 0))
    return pl.pallas_call(
        functools.partial(_conv_kernel, ts),
        grid=(bsz, seq // ts),
        in_specs=[main(), prev(), nxt(), main(), prev(), nxt(), main(),
                  _const_spec(dw_w.shape), _const_spec(dw_b.shape), _const_spec(ln_g.shape),
                  _const_spec(ln_b.shape), _const_spec(sc_w.shape)],
        out_specs=[main(), main()],
        out_shape=[jax.ShapeDtypeStruct((bsz, seq, gw), BF16)] * 2,
        scratch_shapes=[pltpu.VMEM((ts + 2 * CONV_HALO, gw), F32)] * 2,
        compiler_params=_cparams(("parallel", "parallel")),
        name="dwconv",
    )(za, za, za, cv, cv, cv, bg, dw_w, dw_b, ln_g, ln_b, sc_w)


def _s5_kernel(nc, ut_ref, kv_ref, pa_ref, pb_ref, bq1_ref, bq2_ref, qa_ref, qb_ref, c1_ref, c2_ref,
               p1_ref, p2_ref, y_ref, t_ref, bc_ref):
    bsz, pch, _, lc = ut_ref.shape
    rows = bsz * nc
    nst = pa_ref.shape[-1] // 4

    def build(q, carry):
        kvq = kv_ref[pl.ds(pl.multiple_of(q * pch, pch), pch), :]
        for p in range(pch):
            x = jnp.broadcast_to(kvq[p:p + 1, :], (lc, 2 * lc))
            tile = pltpu.roll(x, 0, axis=1, stride=1, stride_axis=0)[:, lc:]
            t_ref[pl.ds(pl.multiple_of(q * lc, lc), lc), p * lc:(p + 1) * lc] = tile.astype(t_ref.dtype)
        return carry

    lax.fori_loop(0, pch, build, 0)

    pa, pb = pa_ref[...], pb_ref[...]
    for q in range(pch):
        bc_ref[q * lc:(q + 1) * lc, :] = (pa * bq1_ref[q:q + 1, :] + pb * bq2_ref[q:q + 1, :]).astype(bc_ref.dtype)

    v = jnp.concatenate([ut_ref[:, q].reshape(rows, lc) for q in range(pch)], axis=1)
    y = jnp.dot(v, t_ref[...], preferred_element_type=F32)
    w = jnp.dot(v, bc_ref[...], preferred_element_type=F32)

    cidx = lax.broadcasted_iota(jnp.int32, (bsz, nc, 2 * nst), 1).reshape(rows, 2 * nst)

    def cmul(x, lvl, half):
        lo, hi = half * 2 * nst, (half + 1) * 2 * nst
        return x * p1_ref[lvl:lvl + 1, lo:hi] + pltpu.roll(x, nst, axis=1) * p2_ref[lvl:lvl + 1, lo:hi]

    def chunk_scan(x, half, reverse):
        lvl, k = 0, 1
        while k < nc:
            sh = pltpu.roll(x, (rows - k) if reverse else k, axis=0)
            keep = (cidx <= nc - 1 - k) if reverse else (cidx >= k)
            x = x + cmul(jnp.where(keep, sh, 0.0), lvl, half)
            lvl, k = lvl + 1, 2 * k
        sh = pltpu.roll(x, (rows - 1) if reverse else 1, axis=0)
        keep = (cidx <= nc - 2) if reverse else (cidx >= 1)
        return jnp.where(keep, sh, 0.0)

    xprev = jnp.concatenate([chunk_scan(w[:, 0:2 * nst], 0, False),
                             chunk_scan(w[:, 2 * nst:4 * nst], 1, True)], axis=1)
    qa, qb = qa_ref[...], qb_ref[...]
    cc = jnp.concatenate([(qa * c1_ref[:, p:p + 1] + qb * c2_ref[:, p:p + 1]).astype(BF16)
                          for p in range(pch)], axis=1)
    y = y + jnp.dot(xprev.astype(BF16), cc, preferred_element_type=F32)
    for p in range(pch):
        y_ref[:, p] = y[:, p * lc:(p + 1) * lc].reshape(bsz, nc, lc)


def _s5_call(layer, ut4, tables):
    bsz, gw, nc, lc = ut4.shape
    ng = tables[0].shape[1]
    pch = gw // ng
    gspec = lambda a: pl.BlockSpec((None, None) + a.shape[2:], lambda g: (layer, g) + (0,) * (a.ndim - 2))
    io = lambda: pl.BlockSpec((bsz, pch, nc, lc), lambda g: (0, g, 0, 0))
    nst4 = tables[1].shape[-1]
    return pl.pallas_call(
        functools.partial(_s5_kernel, nc),
        grid=(ng,),
        in_specs=[io()] + [gspec(a) for a in tables],
        out_specs=io(),
        out_shape=jax.ShapeDtypeStruct((bsz, gw, nc, lc), F32),
        scratch_shapes=[pltpu.VMEM((pch * lc, pch * lc), BF16), pltpu.VMEM((pch * lc, nst4), BF16)],
        compiler_params=_cparams(("parallel",)),
        name="s5_mix",
    )(ut4, *tables)


def _s5_params(a_re, a_im, log_step, b_re, b_im, c_re, c_im, nc):
    lc = S5_CHUNK
    hi = lax.Precision.HIGHEST
    step = jnp.exp(log_step.astype(F32))[..., None]
    lr, li = a_re.astype(F32), a_im.astype(F32)
    mag = jnp.exp(lr * step)
    abr, abi = mag * jnp.cos(li * step), mag * jnp.sin(li * step)
    den = lr * lr + li * li
    pr = abr - 1.0
    fr = (pr * lr + abi * li) / den
    fi = (abi * lr - pr * li) / den
    br, bi = b_re.astype(F32), b_im.astype(F32)
    bbr = fr[..., None] * br - fi[..., None] * bi
    bbi = fr[..., None] * bi + fi[..., None] * br
    cr, ci = c_re.astype(F32), c_im.astype(F32)
    ls, an = lr * step, li * step
    nly, _, ng, nst = ls.shape
    pch = br.shape[-1]

    def apow(ks):
        m = jnp.exp(ls[..., None] * ks)
        return m * jnp.cos(an[..., None] * ks), m * jnp.sin(an[..., None] * ks)

    mr, mi = apow(jnp.arange(lc + 1, dtype=F32))
    mrt, mit = jnp.swapaxes(mr, -1, -2), jnp.swapaxes(mi, -1, -2)

    bt_r, bt_i = jnp.swapaxes(bbr, -1, -2), jnp.swapaxes(bbi, -1, -2)

    def lag_kernels(d):
        cb_r = bt_r[:, d, :, :, None, :] * cr[:, d, :, None] - bt_i[:, d, :, :, None, :] * ci[:, d, :, None]
        cb_i = bt_r[:, d, :, :, None, :] * ci[:, d, :, None] + bt_i[:, d, :, :, None, :] * cr[:, d, :, None]
        cb_r = cb_r.reshape(nly, ng, pch * pch, nst)
        cb_i = cb_i.reshape(nly, ng, pch * pch, nst)
        return (jnp.einsum('lgxn,lgnk->lgxk', cb_r, mr[:, d, :, :, :lc], precision=hi)
                - jnp.einsum('lgxn,lgnk->lgxk', cb_i, mi[:, d, :, :, :lc], precision=hi))

    kf, kb = lag_kernels(0), lag_kernels(1)
    left = jnp.concatenate([jnp.zeros_like(kb[..., :1]), jnp.flip(kb[..., 1:], axis=-1)], axis=-1)
    right = kf.at[..., 0].add(kb[..., 0])
    kv = jnp.concatenate([left, right], axis=-1)

    f_r, f_i = jnp.flip(mrt[:, 0, :, :lc], axis=2), jnp.flip(mit[:, 0, :, :lc], axis=2)
    b_r, b_i = mrt[:, 1, :, :lc], mit[:, 1, :, :lc]
    pa = jnp.concatenate([f_r, f_i, b_r, b_i], axis=-1)
    pb = jnp.concatenate([-f_i, f_r, -b_i, b_r], axis=-1)
    bq1 = jnp.concatenate([bt_r[:, 0], bt_r[:, 0], bt_r[:, 1], bt_r[:, 1]], axis=-1)
    bq2 = jnp.concatenate([bt_i[:, 0], bt_i[:, 0], bt_i[:, 1], bt_i[:, 1]], axis=-1)

    qf_r, qf_i = mr[:, 0, :, :, 1:lc + 1], mi[:, 0, :, :, 1:lc + 1]
    qb_r, qb_i = jnp.flip(mr[:, 1, :, :, 1:lc + 1], axis=-1), jnp.flip(mi[:, 1, :, :, 1:lc + 1], axis=-1)
    qa = jnp.concatenate([qf_r, -qf_i, qb_r, -qb_i], axis=2)
    qb = jnp.concatenate([-qf_i, -qf_r, -qb_i, -qb_r], axis=2)
    ct_r, ct_i = jnp.swapaxes(cr, -1, -2), jnp.swapaxes(ci, -1, -2)
    c1 = jnp.concatenate([ct_r[:, 0], ct_r[:, 0], ct_r[:, 1], ct_r[:, 1]], axis=2)
    c2 = jnp.concatenate([ct_i[:, 0], ct_i[:, 0], ct_i[:, 1], ct_i[:, 1]], axis=2)

    nlev = max(1, (nc - 1).bit_length())
    lv_r, lv_i = apow(float(lc) * 2.0 ** jnp.arange(nlev, dtype=F32))
    lv_r, lv_i = jnp.swapaxes(lv_r, -1, -2), jnp.swapaxes(lv_i, -1, -2)
    p1 = jnp.concatenate([lv_r[:, 0], lv_r[:, 0], lv_r[:, 1], lv_r[:, 1]], axis=-1)
    p2 = jnp.concatenate([-lv_i[:, 0], lv_i[:, 0], -lv_i[:, 1], lv_i[:, 1]], axis=-1)
    return kv, pa, pb, bq1, bq2, qa, qb, c1, c2, p1, p2


def _attn_kernel(tk, lam_ref, q_ref, qn_ref, k_ref, vt_ref, g_ref, o_ref, va_ref, s_ref, mb_ref):
    hd = q_ref.shape[-1]
    dq = hd // 2
    tq = q_ref.shape[2]
    seq = k_ref.shape[2]
    rows_aug = va_ref.shape[0]

    def split_maps(q):
        lane = lax.broadcasted_iota(jnp.int32, q.shape, 1)
        zero = jnp.zeros_like(q)
        return jnp.concatenate([jnp.where(lane < dq, q, zero), jnp.where(lane >= dq, q, zero)], axis=0)

    nblk = seq // tk
    kc = min(tk, 2 * LANES)

    def scores(qq, i, slot):
        kb = k_ref[0, 0, pl.ds(pl.multiple_of(i * tk, tk), tk), :]
        s = lax.dot_general(kb, qq, (((1,), (1,)), ((), ())), preferred_element_type=F32)
        s_ref[slot] = s
        return jnp.max(s, axis=0, keepdims=True)

    qq = split_maps(q_ref[0, 0])

    @pl.when(pl.program_id(2) == 0)
    def _():
        va_ref[0:hd, :] = vt_ref[0, 0]
        row = lax.broadcasted_iota(jnp.int32, (rows_aug - hd, seq), 0)
        va_ref[hd:rows_aug, :] = jnp.where(row == 0, 1.0, 0.0).astype(va_ref.dtype)
        mb_ref[...] = scores(qq, 0, 0)

    def softmax_pv(i, slot, m, mb, acc1, acc2):
        m_new = jnp.maximum(m, mb)
        alpha = jnp.exp2(m - m_new)
        d1 = None
        d2 = None
        for c in range(tk // kc):
            p = jnp.exp2(s_ref[slot, c * kc:(c + 1) * kc, :] - m_new).astype(BF16)
            vb = va_ref[:, pl.ds(pl.multiple_of(i * tk + c * kc, kc), kc)]
            t1 = jnp.dot(vb, p[:, 0:tq], preferred_element_type=F32)
            t2 = jnp.dot(vb, p[:, tq:2 * tq], preferred_element_type=F32)
            d1 = t1 if d1 is None else d1 + t1
            d2 = t2 if d2 is None else d2 + t2
        return m_new, acc1 * alpha[:, 0:tq] + d1, acc2 * alpha[:, tq:2 * tq] + d2

    m = jnp.full((1, 2 * tq), -jnp.inf, F32)
    acc1 = jnp.zeros((rows_aug, tq), F32)
    acc2 = acc1
    assert nblk % 2 == 0
    mb = mb_ref[...]

    def pair(j, carry):
        m, mb, acc1, acc2 = carry
        mb1 = scores(qq, 2 * j + 1, 1)
        m, acc1, acc2 = softmax_pv(2 * j, 0, m, mb, acc1, acc2)
        mb2 = scores(qq, 2 * j + 2, 0)
        m, acc1, acc2 = softmax_pv(2 * j + 1, 1, m, mb1, acc1, acc2)
        return m, mb2, acc1, acc2

    m, mb, acc1, acc2 = lax.fori_loop(0, nblk // 2 - 1, pair, (m, mb, acc1, acc2))
    mb1 = scores(qq, nblk - 1, 1)
    m, acc1, acc2 = softmax_pv(nblk - 2, 0, m, mb, acc1, acc2)
    mb_ref[...] = scores(split_maps(qn_ref[0, 0]), 0, 0)
    m, acc1, acc2 = softmax_pv(nblk - 1, 1, m, mb1, acc1, acc2)
    lam = lam_ref[0]
    post = lam_ref[1]
    o = acc1[0:hd, :] / acc1[hd:hd + 1, :] - lam * (acc2[0:hd, :] / acc2[hd:hd + 1, :])
    o = o * lax.rsqrt(jnp.mean(o * o, axis=0, keepdims=True) + LN_EPS)
    o_ref[0] = (o * g_ref[...] * post).astype(o_ref.dtype)


def _attn_call(lam2, q, k, vt, g_col):
    bsz, nh, seq, hd = q.shape
    tq = min(ATT_Q, seq)
    tk = min(ATT_K, seq)
    rows_aug = hd + 16
    nq = seq // tq
    return pl.pallas_call(
        functools.partial(_attn_kernel, tk),
        grid=(bsz, nh, nq),
        in_specs=[pl.BlockSpec(memory_space=pltpu.SMEM),
                  pl.BlockSpec((1, 1, tq, hd), lambda b, h, i: (b, h, i, 0)),
                  pl.BlockSpec((1, 1, tq, hd), lambda b, h, i: (b, h, jnp.minimum(i + 1, nq - 1), 0)),
                  pl.BlockSpec((1, 1, seq, hd), lambda b, h, i: (b, h, 0, 0)),
                  pl.BlockSpec((1, 1, hd, seq), lambda b, h, i: (b, h, 0, 0)),
                  _const_spec(g_col.shape)],
        out_specs=pl.BlockSpec((1, hd, tq), lambda b, h, i: (b, h, i)),
        out_shape=jax.ShapeDtypeStruct((bsz, nh * hd, seq), BF16),
        scratch_shapes=[pltpu.VMEM((rows_aug, seq), BF16), pltpu.VMEM((2, tk, 2 * tq), F32),
                        pltpu.VMEM((1, 2 * tq), F32)],
        compiler_params=_cparams(("parallel", "parallel", "arbitrary")),
        name="diff_attn",
    )(lam2, q, q, k, vt, g_col)


def _gelu_tanh(x):
    return 0.5 * x * (1.0 + jnp.tanh(math.sqrt(2.0 / math.pi) * (x + 0.044715 * (x * x * x))))


def _ffn_kernel(alpha, splits, x_ref, ya_ref, u_ref, yst_ref, yc_ref, ydt_ref,
                wo_ref, sd_ref, wg_ref, bg_ref, g1_ref, b1_ref, w1_ref, w3_ref, w2_ref,
                g2_ref, b2_ref, o_ref):
    gw = ya_ref.shape[-1]
    x = x_ref[...]
    ys = _gelu_tanh(sd_ref[...] * u_ref[...] + yst_ref[0].T)
    gate = jnp.dot(ys.astype(BF16), wg_ref[...], preferred_element_type=F32) + bg_ref[...]
    y_b = (ys * _sigmoid(gate)).astype(BF16)
    mix = jnp.dot(ya_ref[...], wo_ref[0:gw, :], preferred_element_type=F32)
    mix += jnp.dot(y_b, wo_ref[gw:2 * gw, :], preferred_element_type=F32)
    mix += jnp.dot(yc_ref[...], wo_ref[2 * gw:3 * gw, :], preferred_element_type=F32)
    mix += lax.dot_general(ydt_ref[0], wo_ref[3 * gw:4 * gw, :], (((0,), (0,)), ((), ())),
                           preferred_element_type=F32)
    x1 = _layer_norm(alpha * x + mix, g1_ref[...], b1_ref[...])
    x1b = x1.astype(BF16)
    ff = None
    for lo, hi in splits:
        h1 = jnp.dot(x1b, w1_ref[:, lo:hi], preferred_element_type=F32)
        h3 = jnp.dot(x1b, w3_ref[:, lo:hi], preferred_element_type=F32)
        act = (h1 * _sigmoid(h1) * h3).astype(BF16)
        part = jnp.dot(act, w2_ref[lo:hi, :], preferred_element_type=F32)
        ff = part if ff is None else ff + part
    o_ref[...] = _layer_norm(alpha * x1 + ff, g2_ref[...], b2_ref[...])


def _ffn_call(layer, alpha, x2, ya, u, yst, yc, ydt, wo, sd, wg, bgl, g1, b1, w1, w3, w2, g2, b2, bsz, seq):
    t, d = x2.shape
    gw = d // N_MIXERS
    tm = min(FFN_ROWS, seq)
    ns = seq // tm
    hid = w1.shape[-1]
    cut = (hid // 2 + 255) // 256 * 256
    splits = ((0, cut), (cut, hid)) if 0 < cut < hid else ((0, hid),)
    tok = lambda w: pl.BlockSpec((tm, w), lambda i: (i, 0))
    single = lambda a: _layer_spec(a, layer, pipeline_mode=pl.Buffered(1))
    return pl.pallas_call(
        functools.partial(_ffn_kernel, alpha, splits),
        grid=(t // tm,),
        in_specs=[tok(d), tok(gw), tok(gw),
                  pl.BlockSpec((1, gw, tm), lambda i: (i // ns, 0, i % ns)),
                  tok(gw),
                  pl.BlockSpec((1, gw, tm), lambda i: (i // ns, 0, i % ns)),
                  single(wo), single(sd), single(wg), single(bgl), single(g1), single(b1),
                  single(w1), single(w3), single(w2), single(g2), single(b2)],
        out_specs=tok(d),
        out_shape=jax.ShapeDtypeStruct((t, d), F32),
        compiler_params=_cparams(("parallel",)),
        name="out_ffn",
    )(x2, ya, u, yst, yc, ydt, wo, sd, wg, bgl, g1, b1, w1, w3, w2, g2, b2)


def _rope_tables(seq, gw):
    hd = gw // DA_HEADS
    dq = hd // 2
    rot = dq // 4
    half = rot // 2
    pos = jnp.arange(seq, dtype=F32)
    inv_freq = ROPE_THETA ** (-jnp.arange(0, rot, 2, dtype=F32) / rot)
    ang = pos[:, None] * inv_freq[None, :]
    cos, sin = jnp.cos(ang), jnp.sin(ang)
    lane = jnp.arange(gw) % dq
    idx = lane % half
    lo = (lane < half)[None, :]
    hi = ((lane >= half) & (lane < rot))[None, :]
    cos_l, sin_l = cos[:, idx], sin[:, idx]
    rc = jnp.where(lo | hi, cos_l, 1.0)
    ra = jnp.where(lo, -sin_l, 0.0)
    rb = jnp.where(hi, sin_l, 0.0)
    return rc.astype(F32), ra.astype(F32), rb.astype(F32)


def kernel(x, w_in, w_out, conf_dw_w, conf_dw_b, conf_ln_g, conf_ln_b, s5_a_re, s5_a_im, s5_log_step,
           s5_b_re, s5_b_im, s5_c_re, s5_c_im, s5_d, s5_w_glu, s5_b_glu, sc_conv_w, da_lq1, da_lk1,
           da_lq2, da_lk2, da_subln_g, ln1_g, ln1_b, w_ffn1, w_ffn3, w_ffn2, ln2_g, ln2_b):
    bsz, seq, d = x.shape
    depth = w_in.shape[0]
    gw = d // N_MIXERS
    alpha = (2.0 * depth) ** 0.25
    rc, ra, rb = _rope_tables(seq, gw)
    row = lambda a: a.astype(F32).reshape(1, -1)
    rows = lambda a: a.astype(F32).reshape(depth, 1, -1)
    bf = lambda a: a.astype(BF16)
    w_in, w_out, s5_w_glu, w_ffn1, w_ffn3, w_ffn2 = map(bf, (w_in, w_out, s5_w_glu, w_ffn1, w_ffn3, w_ffn2))
    s5_d, s5_b_glu, ln1_g, ln1_b, ln2_g, ln2_b = map(rows, (s5_d, s5_b_glu, ln1_g, ln1_b, ln2_g, ln2_b))
    nc = seq // S5_CHUNK
    s5p = _s5_params(s5_a_re, s5_a_im, s5_log_step, s5_b_re, s5_b_im, s5_c_re, s5_c_im, nc)
    x2 = x.reshape(bsz * seq, d)
    for l in range(depth):
        za, u, ut, bg, cv, q, k, vt = _proj_call(l, x2, w_in, rc, ra, rb, bsz, seq)
        b3 = lambda a: a.reshape(bsz, seq, gw)
        ya, yc = _conv_call(b3(za), b3(cv), b3(bg), conf_dw_w[l].astype(F32), row(conf_dw_b[l]),
                            row(conf_ln_g[l]), row(conf_ln_b[l]), sc_conv_w[l].astype(F32))
        yst = _s5_call(l, ut.reshape(bsz, gw, nc, S5_CHUNK), s5p).reshape(bsz, gw, seq)
        lam_init = 0.8 - 0.6 * math.exp(-0.3 * l)
        lam = (jnp.exp(jnp.sum(da_lq1[l].astype(F32) * da_lk1[l].astype(F32)))
               - jnp.exp(jnp.sum(da_lq2[l].astype(F32) * da_lk2[l].astype(F32))) + lam_init)
        lam2 = jnp.stack([lam, jnp.asarray(1.0 - lam_init, F32)]).astype(F32)
        ydt = _attn_call(lam2, q, k, vt, da_subln_g[l].astype(F32).reshape(-1, 1))
        x2 = _ffn_call(l, alpha, x2, ya.reshape(-1, gw), u, yst, yc.reshape(-1, gw), ydt,
                       w_out, s5_d, s5_w_glu, s5_b_glu, ln1_g, ln1_b, w_ffn1, w_ffn3, w_ffn2, ln2_g, ln2_b,
                       bsz, seq)
    return x2.reshape(bsz, seq, d)
```

```python
import functools
import math

import jax
import jax.numpy as jnp
from jax import lax
from jax.experimental import pallas as pl
from jax.experimental.pallas import tpu as pltpu

F32 = jnp.float32
BF16 = jnp.bfloat16

N_MIXERS = 4
CONV_K = 31
S5_CH = 16
S5_STATE = 64
SHORT_K = 3
DA_HEADS = 4
ROPE_THETA = 500000.0
LN_EPS = 1e-5

SUBLANES = 8
LANES = 128
VMEM_LIMIT_BYTES = 56 * 1024 * 1024

PROJ_ROWS = 512
CONV_ROWS = 512
CONV_HALO = 16
CONV_CHUNK = 128
S5_CHUNK = LANES
ATT_Q = 256
ATT_K = 1024
FFN_ROWS = 512


def _cparams(sem):
    return pltpu.CompilerParams(dimension_semantics=sem, vmem_limit_bytes=VMEM_LIMIT_BYTES)


def _const_spec(shape):
    nd = len(shape)
    return pl.BlockSpec(shape, lambda *_: (0,) * nd)


def _layer_spec(a, layer, **kw):
    return pl.BlockSpec((None,) + a.shape[1:], lambda *_: (layer,) + (0,) * (a.ndim - 1), **kw)


def _sigmoid(x):
    return 1.0 / (1.0 + jnp.exp(-x))


def _layer_norm(x, g, b):
    mu = jnp.mean(x, axis=-1, keepdims=True)
    xc = x - mu
    var = jnp.mean(xc * xc, axis=-1, keepdims=True)
    return xc * lax.rsqrt(var + LN_EPS) * g + b


def _proj_kernel(gw, x_ref, w_ref, rc_ref, ra_ref, rb_ref,
                 za_ref, u_ref, ut_ref, bg_ref, cv_ref, q_ref, k_ref, vt_ref):
    xb = x_ref[...].astype(BF16)

    def seg(i):
        return jnp.dot(xb, w_ref[:, i * gw:(i + 1) * gw], preferred_element_type=F32)

    za_ref[...] = seg(0) * _sigmoid(seg(1))
    u = seg(2)
    u_ref[...] = u
    ut_ref[0] = u.T.astype(BF16)
    bg_ref[...] = seg(3)
    cv_ref[...] = seg(4) * seg(5)

    rc, ra, rb = rc_ref[...], ra_ref[...], rb_ref[...]
    hd = gw // DA_HEADS
    half = hd // 2 // 4 // 2

    def rope(t):
        return t * rc + pltpu.roll(t, gw - half, axis=1) * ra + pltpu.roll(t, half, axis=1) * rb

    scale = (hd // 2) ** -0.5 * math.log2(math.e)
    q = rope(seg(6)) * scale
    k = rope(seg(7))
    vt = seg(8).T
    for h in range(DA_HEADS):
        q_ref[0, h] = q[:, h * hd:(h + 1) * hd].astype(BF16)
        k_ref[0, h] = k[:, h * hd:(h + 1) * hd].astype(BF16)
        vt_ref[0, h] = vt[h * hd:(h + 1) * hd, :].astype(BF16)


def _proj_call(layer, x2, w_in, rc, ra, rb, bsz, seq):
    t, d = x2.shape
    gw = d // N_MIXERS
    tm = min(PROJ_ROWS, seq)
    ns = seq // tm
    hd = gw // DA_HEADS
    tok = lambda: pl.BlockSpec((tm, gw), lambda i: (i, 0))
    rope_spec = lambda: pl.BlockSpec((tm, gw), lambda i: (i % ns, 0))
    out_shape = (
        [jax.ShapeDtypeStruct((t, gw), F32)] * 2
        + [jax.ShapeDtypeStruct((bsz, gw, seq), BF16)]
        + [jax.ShapeDtypeStruct((t, gw), F32)] * 2
        + [jax.ShapeDtypeStruct((bsz, DA_HEADS, seq, hd), BF16)] * 2
        + [jax.ShapeDtypeStruct((bsz, DA_HEADS, hd, seq), BF16)]
    )
    return pl.pallas_call(
        functools.partial(_proj_kernel, gw),
        grid=(t // tm,),
        in_specs=[pl.BlockSpec((tm, d), lambda i: (i, 0)), _layer_spec(w_in, layer),
                  rope_spec(), rope_spec(), rope_spec()],
        out_specs=[tok(), tok(), pl.BlockSpec((1, gw, tm), lambda i: (i // ns, 0, i % ns)), tok(), tok(),
                   pl.BlockSpec((1, DA_HEADS, tm, hd), lambda i: (i // ns, 0, i % ns, 0)),
                   pl.BlockSpec((1, DA_HEADS, tm, hd), lambda i: (i // ns, 0, i % ns, 0)),
                   pl.BlockSpec((1, DA_HEADS, hd, tm), lambda i: (i // ns, 0, 0, i % ns))],
        out_shape=out_shape,
        compiler_params=_cparams(("parallel",)),
        name="proj_in",
    )(x2, w_in, rc, ra, rb)


def _conv_kernel(ts, za_ref, zp_ref, zn_ref, cv_ref, cp_ref, cn_ref, bg_ref,
                 dw_ref, db_ref, g_ref, b_ref, sw_ref, ya_ref, yc_ref, zbuf, cbuf):
    j = pl.program_id(1)
    nj = pl.num_programs(1)
    keep_prev = jnp.where(j > 0, 1.0, 0.0).astype(F32)
    keep_next = jnp.where(j < nj - 1, 1.0, 0.0).astype(F32)
    h = CONV_HALO
    zbuf[0:h, :] = zp_ref[0] * keep_prev
    zbuf[h:h + ts, :] = za_ref[0]
    zbuf[h + ts:2 * h + ts, :] = zn_ref[0] * keep_next
    cbuf[0:h, :] = cp_ref[0] * keep_prev
    cbuf[h:h + ts, :] = cv_ref[0]
    cbuf[h + ts:2 * h + ts, :] = cn_ref[0] * keep_next

    rows = CONV_CHUNK
    win_rows = rows + 2 * h

    def taps(buf, base, w_ref, ntaps):
        win = buf[pl.ds(base, win_rows), :]
        first = h - ntaps // 2
        acc = jnp.zeros((rows, win.shape[-1]), F32)
        for phase in range(SUBLANES):
            offs = [o for o in range(first, first + ntaps) if o % SUBLANES == phase]
            if not offs:
                continue
            rolled = win if phase == 0 else pltpu.roll(win, win_rows - phase, axis=0)
            for o in offs:
                lo = o - phase
                acc = acc + w_ref[o - first:o - first + 1, :] * rolled[lo:lo + rows, :]
        return acc

    def chunk(c, carry):
        base = pl.multiple_of(c * rows, rows)
        acc = taps(zbuf, base, dw_ref, CONV_K)
        z = _layer_norm(acc + db_ref[...], g_ref[...], b_ref[...])
        ya_ref[0, pl.ds(base, rows), :] = (z * _sigmoid(z)).astype(ya_ref.dtype)
        sc = taps(cbuf, base, sw_ref, SHORT_K)
        yc_ref[0, pl.ds(base, rows), :] = (bg_ref[0, pl.ds(base, rows), :] * sc).astype(yc_ref.dtype)
        return carry

    lax.fori_loop(0, ts // rows, chunk, 0)


def _conv_call(za, cv, bg, dw_w, dw_b, ln_g, ln_b, sc_w):
    bsz, seq, gw = za.shape
    ts = min(CONV_ROWS, seq)
    r = ts // CONV_HALO
    nh = seq // CONV_HALO
    main = lambda: pl.BlockSpec((1, ts, gw), lambda b, j: (b, j, 0))
    prev = lambda: pl.BlockSpec((1, CONV_HALO, gw), lambda b, j: (b, jnp.maximum(j * r - 1, 0), 0))
    nxt = lambda: pl.BlockSpec((1, CONV_HALO, gw), lambda b, j: (b, jnp.minimum((j + 1) * r, nh - 1), 0))
    return pl.pallas_call(
        functools.partial(_conv_kernel, ts),
        grid=(bsz, seq // ts),
        in_specs=[main(), prev(), nxt(), main(), prev(), nxt(), main(),
                  _const_spec(dw_w.shape), _const_spec(dw_b.shape), _const_spec(ln_g.shape),
                  _const_spec(ln_b.shape), _const_spec(sc_w.shape)],
        out_specs=[main(), main()],
        out_shape=[jax.ShapeDtypeStruct((bsz, seq, gw), BF16)] * 2,
        scratch_shapes=[pltpu.VMEM((ts + 2 * CONV_HALO, gw), F32)] * 2,
        compiler_params=_cparams(("parallel", "parallel")),
        name="dwconv",
    )(za, za, za, cv, cv, cv, bg, dw_w, dw_b, ln_g, ln_b, sc_w)


def _s5_kernel(nc, ut_ref, kv_ref, pa_ref, pb_ref, bq1_ref, bq2_ref, qa_ref, qb_ref, c1_ref, c2_ref,
               p1_ref, p2_ref, y_ref, t_ref, bc_ref):
    bsz, pch, _, lc = ut_ref.shape
    rows = bsz * nc
    nst = pa_ref.shape[-1] // 4

    def build(q, carry):
        kvq = kv_ref[pl.ds(pl.multiple_of(q * pch, pch), pch), :]
        for p in range(pch):
            x = jnp.broadcast_to(kvq[p:p + 1, :], (lc, 2 * lc))
            tile = pltpu.roll(x, 0, axis=1, stride=1, stride_axis=0)[:, lc:]
            t_ref[pl.ds(pl.multiple_of(q * lc, lc), lc), p * lc:(p + 1) * lc] = tile.astype(t_ref.dtype)
        return carry

    lax.fori_loop(0, pch, build, 0)

    pa, pb = pa_ref[...], pb_ref[...]
    for q in range(pch):
        bc_ref[q * lc:(q + 1) * lc, :] = (pa * bq1_ref[q:q + 1, :] + pb * bq2_ref[q:q + 1, :]).astype(bc_ref.dtype)

    v = jnp.concatenate([ut_ref[:, q].reshape(rows, lc) for q in range(pch)], axis=1)
    y = jnp.dot(v, t_ref[...], preferred_element_type=F32)
    w = jnp.dot(v, bc_ref[...], preferred_element_type=F32)

    cidx = lax.broadcasted_iota(jnp.int32, (bsz, nc, 2 * nst), 1).reshape(rows, 2 * nst)

    def cmul(x, lvl, half):
        lo, hi = half * 2 * nst, (half + 1) * 2 * nst
        return x * p1_ref[lvl:lvl + 1, lo:hi] + pltpu.roll(x, nst, axis=1) * p2_ref[lvl:lvl + 1, lo:hi]

    def chunk_scan(x, half, reverse):
        lvl, k = 0, 1
        while k < nc:
            sh = pltpu.roll(x, (rows - k) if reverse else k, axis=0)
            keep = (cidx <= nc - 1 - k) if reverse else (cidx >= k)
            x = x + cmul(jnp.where(keep, sh, 0.0), lvl, half)
            lvl, k = lvl + 1, 2 * k
        sh = pltpu.roll(x, (rows - 1) if reverse else 1, axis=0)
        keep = (cidx <= nc - 2) if reverse else (cidx >= 1)
        return jnp.where(keep, sh, 0.0)

    xprev = jnp.concatenate([chunk_scan(w[:, 0:2 * nst], 0, False),
                             chunk_scan(w[:, 2 * nst:4 * nst], 1, True)], axis=1)
    qa, qb = qa_ref[...], qb_ref[...]
    cc = jnp.concatenate([(qa * c1_ref[:, p:p + 1] + qb * c2_ref[:, p:p + 1]).astype(BF16)
                          for p in range(pch)], axis=1)
    y = y + jnp.dot(xprev.astype(BF16), cc, preferred_element_type=F32)
    for p in range(pch):
        y_ref[:, p] = y[:, p * lc:(p + 1) * lc].reshape(bsz, nc, lc)


def _s5_call(layer, ut4, tables):
    bsz, gw, nc, lc = ut4.shape
    ng = tables[0].shape[1]
    pch = gw // ng
    gspec = lambda a: pl.BlockSpec((None, None) + a.shape[2:], lambda g: (layer, g) + (0,) * (a.ndim - 2))
    io = lambda: pl.BlockSpec((bsz, pch, nc, lc), lambda g: (0, g, 0, 0))
    nst4 = tables[1].shape[-1]
    return pl.pallas_call(
        functools.partial(_s5_kernel, nc),
        grid=(ng,),
        in_specs=[io()] + [gspec(a) for a in tables],
        out_specs=io(),
        out_shape=jax.ShapeDtypeStruct((bsz, gw, nc, lc), F32),
        scratch_shapes=[pltpu.VMEM((pch * lc, pch * lc), BF16), pltpu.VMEM((pch * lc, nst4), BF16)],
        compiler_params=_cparams(("parallel",)),
        name="s5_mix",
    )(ut4, *tables)


def _s5_params(a_re, a_im, log_step, b_re, b_im, c_re, c_im, nc):
    lc = S5_CHUNK
    hi = lax.Precision.HIGHEST
    step = jnp.exp(log_step.astype(F32))[..., None]
    lr, li = a_re.astype(F32), a_im.astype(F32)
    mag = jnp.exp(lr * step)
    abr, abi = mag * jnp.cos(li * step), mag * jnp.sin(li * step)
    den = lr * lr + li * li
    pr = abr - 1.0
    fr = (pr * lr + abi * li) / den
    fi = (abi * lr - pr * li) / den
    br, bi = b_re.astype(F32), b_im.astype(F32)
    bbr = fr[..., None] * br - fi[..., None] * bi
    bbi = fr[..., None] * bi + fi[..., None] * br
    cr, ci = c_re.astype(F32), c_im.astype(F32)
    ls, an = lr * step, li * step
    nly, _, ng, nst = ls.shape
    pch = br.shape[-1]

    def apow(d, ks):
        m = jnp.exp(ls[:, d, :, :, None] * ks)
        return m * jnp.cos(an[:, d, :, :, None] * ks), m * jnp.sin(an[:, d, :, :, None] * ks)

    def apow_t(d, ks):
        m = jnp.exp(ks[:, None] * ls[:, d, :, None, :])
        return m * jnp.cos(ks[:, None] * an[:, d, :, None, :]), m * jnp.sin(ks[:, None] * an[:, d, :, None, :])

    kk = jnp.arange(lc, dtype=F32)
    bt_r, bt_i = jnp.swapaxes(bbr, -1, -2), jnp.swapaxes(bbi, -1, -2)

    def c_times_b(d):
        cb_r = bt_r[:, d, :, :, None, :] * cr[:, d, :, None] - bt_i[:, d, :, :, None, :] * ci[:, d, :, None]
        cb_i = bt_r[:, d, :, :, None, :] * ci[:, d, :, None] + bt_i[:, d, :, :, None, :] * cr[:, d, :, None]
        return cb_r.reshape(nly, ng, pch * pch, nst), cb_i.reshape(nly, ng, pch * pch, nst)

    def lag_kernels(cb, pw):
        return (jnp.einsum('lgxn,lgnk->lgxk', cb[0], pw[0], precision=hi)
                - jnp.einsum('lgxn,lgnk->lgxk', cb[1], pw[1], precision=hi))

    cb_f, cb_b = c_times_b(0), c_times_b(1)
    qb_r, qb_i = apow(1, lc - kk)
    left = jnp.where(kk > 0, lag_kernels(cb_b, (qb_r, qb_i)), 0.0)
    right = lag_kernels(cb_f, apow(0, kk)) + jnp.where(kk == 0, jnp.sum(cb_b[0], axis=-1, keepdims=True), 0.0)
    kv = jnp.concatenate([left, right], axis=-1)

    f_r, f_i = apow_t(0, lc - 1 - kk)
    b_r, b_i = apow_t(1, kk)
    pa = jnp.concatenate([f_r, f_i, b_r, b_i], axis=-1)
    pb = jnp.concatenate([-f_i, f_r, -b_i, b_r], axis=-1)
    bq1 = jnp.concatenate([bt_r[:, 0], bt_r[:, 0], bt_r[:, 1], bt_r[:, 1]], axis=-1)
    bq2 = jnp.concatenate([bt_i[:, 0], bt_i[:, 0], bt_i[:, 1], bt_i[:, 1]], axis=-1)

    qf_r, qf_i = apow(0, kk + 1.0)
    qa = jnp.concatenate([qf_r, -qf_i, qb_r, -qb_i], axis=2)
    qb = jnp.concatenate([-qf_i, -qf_r, -qb_i, -qb_r], axis=2)
    ct_r, ct_i = jnp.swapaxes(cr, -1, -2), jnp.swapaxes(ci, -1, -2)
    c1 = jnp.concatenate([ct_r[:, 0], ct_r[:, 0], ct_r[:, 1], ct_r[:, 1]], axis=2)
    c2 = jnp.concatenate([ct_i[:, 0], ct_i[:, 0], ct_i[:, 1], ct_i[:, 1]], axis=2)

    nlev = max(1, (nc - 1).bit_length())
    lev = float(lc) * 2.0 ** jnp.arange(nlev, dtype=F32)
    (lf_r, lf_i), (lb_r, lb_i) = apow_t(0, lev), apow_t(1, lev)
    p1 = jnp.concatenate([lf_r, lf_r, lb_r, lb_r], axis=-1)
    p2 = jnp.concatenate([-lf_i, lf_i, -lb_i, lb_i], axis=-1)
    return kv, pa, pb, bq1, bq2, qa, qb, c1, c2, p1, p2


def _attn_kernel(tk, lam_ref, q_ref, qn_ref, k_ref, vt_ref, g_ref, o_ref, va_ref, s_ref, mb_ref):
    hd = q_ref.shape[-1]
    dq = hd // 2
    tq = q_ref.shape[2]
    seq = k_ref.shape[2]
    rows_aug = va_ref.shape[0]

    def split_maps(q):
        lane = lax.broadcasted_iota(jnp.int32, q.shape, 1)
        zero = jnp.zeros_like(q)
        return jnp.concatenate([jnp.where(lane < dq, q, zero), jnp.where(lane >= dq, q, zero)], axis=0)

    nblk = seq // tk
    kc = min(tk, 2 * LANES)

    def scores(qq, i, slot):
        kb = k_ref[0, 0, pl.ds(pl.multiple_of(i * tk, tk), tk), :]
        s = lax.dot_general(kb, qq, (((1,), (1,)), ((), ())), preferred_element_type=F32)
        s_ref[slot] = s
        return jnp.max(s, axis=0, keepdims=True)

    qq = split_maps(q_ref[0, 0])

    @pl.when(pl.program_id(2) == 0)
    def _():
        va_ref[0:hd, :] = vt_ref[0, 0]
        row = lax.broadcasted_iota(jnp.int32, (rows_aug - hd, seq), 0)
        va_ref[hd:rows_aug, :] = jnp.where(row == 0, 1.0, 0.0).astype(va_ref.dtype)
        mb_ref[...] = scores(qq, 0, 0)

    def softmax_pv(i, slot, m, mb, acc1, acc2):
        m_new = jnp.maximum(m, mb)
        alpha = jnp.exp2(m - m_new)
        d1 = None
        d2 = None
        for c in range(tk // kc):
            p = jnp.exp2(s_ref[slot, c * kc:(c + 1) * kc, :] - m_new).astype(BF16)
            vb = va_ref[:, pl.ds(pl.multiple_of(i * tk + c * kc, kc), kc)]
            t1 = jnp.dot(vb, p[:, 0:tq], preferred_element_type=F32)
            t2 = jnp.dot(vb, p[:, tq:2 * tq], preferred_element_type=F32)
            d1 = t1 if d1 is None else d1 + t1
            d2 = t2 if d2 is None else d2 + t2
        return m_new, acc1 * alpha[:, 0:tq] + d1, acc2 * alpha[:, tq:2 * tq] + d2

    m = jnp.full((1, 2 * tq), -jnp.inf, F32)
    acc1 = jnp.zeros((rows_aug, tq), F32)
    acc2 = acc1
    assert nblk % 2 == 0
    mb = mb_ref[...]

    def pair(j, carry):
        m, mb, acc1, acc2 = carry
        mb1 = scores(qq, 2 * j + 1, 1)
        m, acc1, acc2 = softmax_pv(2 * j, 0, m, mb, acc1, acc2)
        mb2 = scores(qq, 2 * j + 2, 0)
        m, acc1, acc2 = softmax_pv(2 * j + 1, 1, m, mb1, acc1, acc2)
        return m, mb2, acc1, acc2

    m, mb, acc1, acc2 = lax.fori_loop(0, nblk // 2 - 1, pair, (m, mb, acc1, acc2))
    mb1 = scores(qq, nblk - 1, 1)
    m, acc1, acc2 = softmax_pv(nblk - 2, 0, m, mb, acc1, acc2)
    mb_ref[...] = scores(split_maps(qn_ref[0, 0]), 0, 0)
    m, acc1, acc2 = softmax_pv(nblk - 1, 1, m, mb1, acc1, acc2)
    lam = lam_ref[0]
    post = lam_ref[1]
    o = acc1[0:hd, :] / acc1[hd:hd + 1, :] - lam * (acc2[0:hd, :] / acc2[hd:hd + 1, :])
    o = o * lax.rsqrt(jnp.mean(o * o, axis=0, keepdims=True) + LN_EPS)
    o_ref[0] = (o * g_ref[...] * post).astype(o_ref.dtype)


def _attn_call(lam2, q, k, vt, g_col):
    bsz, nh, seq, hd = q.shape
    tq = min(ATT_Q, seq)
    tk = min(ATT_K, seq)
    rows_aug = hd + 16
    nq = seq // tq
    return pl.pallas_call(
        functools.partial(_attn_kernel, tk),
        grid=(bsz, nh, nq),
        in_specs=[pl.BlockSpec(memory_space=pltpu.SMEM),
                  pl.BlockSpec((1, 1, tq, hd), lambda b, h, i: (b, h, i, 0)),
                  pl.BlockSpec((1, 1, tq, hd), lambda b, h, i: (b, h, jnp.minimum(i + 1, nq - 1), 0)),
                  pl.BlockSpec((1, 1, seq, hd), lambda b, h, i: (b, h, 0, 0)),
                  pl.BlockSpec((1, 1, hd, seq), lambda b, h, i: (b, h, 0, 0)),
                  _const_spec(g_col.shape)],
        out_specs=pl.BlockSpec((1, hd, tq), lambda b, h, i: (b, h, i)),
        out_shape=jax.ShapeDtypeStruct((bsz, nh * hd, seq), BF16),
        scratch_shapes=[pltpu.VMEM((rows_aug, seq), BF16), pltpu.VMEM((2, tk, 2 * tq), F32),
                        pltpu.VMEM((1, 2 * tq), F32)],
        compiler_params=_cparams(("parallel", "parallel", "arbitrary")),
        name="diff_attn",
    )(lam2, q, q, k, vt, g_col)


def _gelu_tanh(x):
    return 0.5 * x * (1.0 + jnp.tanh(math.sqrt(2.0 / math.pi) * (x + 0.044715 * (x * x * x))))


def _ffn_kernel(alpha, splits, x_ref, ya_ref, u_ref, yst_ref, yc_ref, ydt_ref,
                wo_ref, sd_ref, wg_ref, bg_ref, g1_ref, b1_ref, w1_ref, w3_ref, w2_ref,
                g2_ref, b2_ref, o_ref):
    gw = ya_ref.shape[-1]
    x = x_ref[...]
    ys = _gelu_tanh(sd_ref[...] * u_ref[...] + yst_ref[0].T)
    gate = jnp.dot(ys.astype(BF16), wg_ref[...], preferred_element_type=F32) + bg_ref[...]
    y_b = (ys * _sigmoid(gate)).astype(BF16)
    mix = jnp.dot(ya_ref[...], wo_ref[0:gw, :], preferred_element_type=F32)
    mix += jnp.dot(y_b, wo_ref[gw:2 * gw, :], preferred_element_type=F32)
    mix += jnp.dot(yc_ref[...], wo_ref[2 * gw:3 * gw, :], preferred_element_type=F32)
    mix += lax.dot_general(ydt_ref[0], wo_ref[3 * gw:4 * gw, :], (((0,), (0,)), ((), ())),
                           preferred_element_type=F32)
    x1 = _layer_norm(alpha * x + mix, g1_ref[...], b1_ref[...])
    x1b = x1.astype(BF16)
    ff = None
    for lo, hi in splits:
        h1 = jnp.dot(x1b, w1_ref[:, lo:hi], preferred_element_type=F32)
        h3 = jnp.dot(x1b, w3_ref[:, lo:hi], preferred_element_type=F32)
        act = (h1 * _sigmoid(h1) * h3).astype(BF16)
        part = jnp.dot(act, w2_ref[lo:hi, :], preferred_element_type=F32)
        ff = part if ff is None else ff + part
    o_ref[...] = _layer_norm(alpha * x1 + ff, g2_ref[...], b2_ref[...])


def _ffn_call(layer, alpha, x2, ya, u, yst, yc, ydt, wo, sd, wg, bgl, g1, b1, w1, w3, w2, g2, b2, bsz, seq):
    t, d = x2.shape
    gw = d // N_MIXERS
    tm = min(FFN_ROWS, seq)
    ns = seq // tm
    hid = w1.shape[-1]
    cut = (hid // 2 + 255) // 256 * 256
    splits = ((0, cut), (cut, hid)) if 0 < cut < hid else ((0, hid),)
    tok = lambda w: pl.BlockSpec((tm, w), lambda i: (i, 0))
    single = lambda a: _layer_spec(a, layer, pipeline_mode=pl.Buffered(1))
    return pl.pallas_call(
        functools.partial(_ffn_kernel, alpha, splits),
        grid=(t // tm,),
        in_specs=[tok(d), tok(gw), tok(gw),
                  pl.BlockSpec((1, gw, tm), lambda i: (i // ns, 0, i % ns)),
                  tok(gw),
                  pl.BlockSpec((1, gw, tm), lambda i: (i // ns, 0, i % ns)),
                  single(wo), single(sd), single(wg), single(bgl), single(g1), single(b1),
                  single(w1), single(w3), single(w2), single(g2), single(b2)],
        out_specs=tok(d),
        out_shape=jax.ShapeDtypeStruct((t, d), F32),
        compiler_params=_cparams(("parallel",)),
        name="out_ffn",
    )(x2, ya, u, yst, yc, ydt, wo, sd, wg, bgl, g1, b1, w1, w3, w2, g2, b2)


def _rope_tables(seq, gw):
    hd = gw // DA_HEADS
    dq = hd // 2
    rot = dq // 4
    half = rot // 2
    pos = jnp.arange(seq, dtype=F32)
    inv_freq = ROPE_THETA ** (-jnp.arange(0, rot, 2, dtype=F32) / rot)
    ang = pos[:, None] * inv_freq[None, :]
    cos, sin = jnp.cos(ang), jnp.sin(ang)
    lane = jnp.arange(gw) % dq
    idx = lane % half
    lo = (lane < half)[None, :]
    hi = ((lane >= half) & (lane < rot))[None, :]
    cos_l, sin_l = cos[:, idx], sin[:, idx]
    rc = jnp.where(lo | hi, cos_l, 1.0)
    ra = jnp.where(lo, -sin_l, 0.0)
    rb = jnp.where(hi, sin_l, 0.0)
    return rc.astype(F32), ra.astype(F32), rb.astype(F32)


def kernel(x, w_in, w_out, conf_dw_w, conf_dw_b, conf_ln_g, conf_ln_b, s5_a_re, s5_a_im, s5_log_step,
           s5_b_re, s5_b_im, s5_c_re, s5_c_im, s5_d, s5_w_glu, s5_b_glu, sc_conv_w, da_lq1, da_lk1,
           da_lq2, da_lk2, da_subln_g, ln1_g, ln1_b, w_ffn1, w_ffn3, w_ffn2, ln2_g, ln2_b):
    bsz, seq, d = x.shape
    depth = w_in.shape[0]
    gw = d // N_MIXERS
    alpha = (2.0 * depth) ** 0.25
    rc, ra, rb = _rope_tables(seq, gw)
    row = lambda a: a.astype(F32).reshape(1, -1)
    rows = lambda a: a.astype(F32).reshape(depth, 1, -1)
    bf = lambda a: a.astype(BF16)
    w_in, w_out, s5_w_glu, w_ffn1, w_ffn3, w_ffn2 = map(bf, (w_in, w_out, s5_w_glu, w_ffn1, w_ffn3, w_ffn2))
    s5_d, s5_b_glu, ln1_g, ln1_b, ln2_g, ln2_b = map(rows, (s5_d, s5_b_glu, ln1_g, ln1_b, ln2_g, ln2_b))
    nc = seq // S5_CHUNK
    s5p = _s5_params(s5_a_re, s5_a_im, s5_log_step, s5_b_re, s5_b_im, s5_c_re, s5_c_im, nc)
    x2 = x.reshape(bsz * seq, d)
    for l in range(depth):
        za, u, ut, bg, cv, q, k, vt = _proj_call(l, x2, w_in, rc, ra, rb, bsz, seq)
        b3 = lambda a: a.reshape(bsz, seq, gw)
        ya, yc = _conv_call(b3(za), b3(cv), b3(bg), conf_dw_w[l].astype(F32), row(conf_dw_b[l]),
                            row(conf_ln_g[l]), row(conf_ln_b[l]), sc_conv_w[l].astype(F32))
        yst = _s5_call(l, ut.reshape(bsz, gw, nc, S5_CHUNK), s5p).reshape(bsz, gw, seq)
        lam_init = 0.8 - 0.6 * math.exp(-0.3 * l)
        lam = (jnp.exp(jnp.sum(da_lq1[l].astype(F32) * da_lk1[l].astype(F32)))
               - jnp.exp(jnp.sum(da_lq2[l].astype(F32) * da_lk2[l].astype(F32))) + lam_init)
        lam2 = jnp.stack([lam, jnp.asarray(1.0 - lam_init, F32)]).astype(F32)
        ydt = _attn_call(lam2, q, k, vt, da_subln_g[l].astype(F32).reshape(-1, 1))
        x2 = _ffn_call(l, alpha, x2, ya.reshape(-1, gw), u, yst, yc.reshape(-1, gw), ydt,
                       w_out, s5_d, s5_w_glu, s5_b_glu, ln1_g, ln1_b, w_ffn1, w_ffn3, w_ffn2, ln2_g, ln2_b,
                       bsz, seq)
    return x2.reshape(bsz, seq, d)
```

```python
import functools
import math

import jax
import jax.numpy as jnp
from jax import lax
from jax.experimental import pallas as pl
from jax.experimental.pallas import tpu as pltpu

F32 = jnp.float32
BF16 = jnp.bfloat16

N_MIXERS = 4
CONV_K = 31
S5_CH = 16
S5_STATE = 64
SHORT_K = 3
DA_HEADS = 4
ROPE_THETA = 500000.0
LN_EPS = 1e-5

SUBLANES = 8
LANES = 128
VMEM_LIMIT_BYTES = 56 * 1024 * 1024

PROJ_ROWS = 512
CONV_HALO = 16
CONV_CHUNK = 128
S5_CHUNK = LANES
ATT_Q = 256
ATT_K = 1024
FFN_ROWS = 512


def _cparams(sem):
    return pltpu.CompilerParams(dimension_semantics=sem, vmem_limit_bytes=VMEM_LIMIT_BYTES)


def _const_spec(shape):
    nd = len(shape)
    return pl.BlockSpec(shape, lambda *_: (0,) * nd)


def _layer_spec(a, layer, **kw):
    return pl.BlockSpec((None,) + a.shape[1:], lambda *_: (layer,) + (0,) * (a.ndim - 1), **kw)


def _sigmoid(x):
    return 1.0 / (1.0 + jnp.exp(-x))


def _layer_norm(x, g, b):
    mu = jnp.mean(x, axis=-1, keepdims=True)
    xc = x - mu
    var = jnp.mean(xc * xc, axis=-1, keepdims=True)
    return xc * lax.rsqrt(var + LN_EPS) * g + b


def _conv_taps(buf, base, rows, w_ref, ntaps):
    win_rows = rows + 2 * CONV_HALO
    win = buf[base:base + win_rows, :]
    first = CONV_HALO - ntaps // 2
    acc = jnp.zeros((rows, win.shape[-1]), F32)
    for phase in range(SUBLANES):
        offs = [o for o in range(first, first + ntaps) if o % SUBLANES == phase]
        if not offs:
            continue
        rolled = win if phase == 0 else pltpu.roll(win, win_rows - phase, axis=0)
        for o in offs:
            lo = o - phase
            acc = acc + w_ref[o - first:o - first + 1, :] * rolled[lo:lo + rows, :]
    return acc


def _proj_kernel(gw, tiles_per_seq, x_ref, w_ref, rc_ref, ra_ref, rb_ref, dw_ref, db_ref, g_ref, b_ref, sw_ref,
                 u_ref, ut_ref, q_ref, k_ref, vt_ref, ya_ref, yc_ref, zwin, cwin, bghold):
    i = pl.program_id(0)
    tm = x_ref.shape[0]
    h = CONV_HALO
    xb = x_ref[...].astype(BF16)

    def seg(j):
        return jnp.dot(xb, w_ref[:, j * gw:(j + 1) * gw], preferred_element_type=F32)

    @pl.when(i == 0)
    def _():
        zwin[...] = jnp.zeros_like(zwin)
        cwin[...] = jnp.zeros_like(cwin)
        bghold[...] = jnp.zeros_like(bghold)

    za = seg(0) * _sigmoid(seg(1))
    bg = seg(3)
    cv = seg(4) * seg(5)

    keep_next = jnp.where(i % tiles_per_seq != 0, 1.0, 0.0).astype(F32)
    zwin[h + tm:2 * h + tm, :] = za[0:h, :] * keep_next
    cwin[h + tm:2 * h + tm, :] = cv[0:h, :] * keep_next
    rows = CONV_CHUNK
    for c in range(tm // rows):
        base = c * rows
        z = _layer_norm(_conv_taps(zwin, base, rows, dw_ref, CONV_K) + db_ref[...], g_ref[...], b_ref[...])
        ya_ref[base:base + rows, :] = (z * _sigmoid(z)).astype(ya_ref.dtype)
        sc = _conv_taps(cwin, base, rows, sw_ref, SHORT_K)
        yc_ref[base:base + rows, :] = (bghold[base:base + rows, :] * sc).astype(yc_ref.dtype)
    zwin[0:h, :] = zwin[tm:tm + h, :] * keep_next
    cwin[0:h, :] = cwin[tm:tm + h, :] * keep_next
    zwin[h:h + tm, :] = za
    cwin[h:h + tm, :] = cv
    bghold[...] = bg

    u = seg(2)
    u_ref[...] = u
    ut_ref[0] = u.T.astype(BF16)

    rc, ra, rb = rc_ref[...], ra_ref[...], rb_ref[...]
    hd = gw // DA_HEADS
    half = hd // 2 // 4 // 2

    def rope(t):
        return t * rc + pltpu.roll(t, gw - half, axis=1) * ra + pltpu.roll(t, half, axis=1) * rb

    scale = (hd // 2) ** -0.5 * math.log2(math.e)
    q = rope(seg(6)) * scale
    k = rope(seg(7))
    vt = seg(8).T
    for hh in range(DA_HEADS):
        q_ref[0, hh] = q[:, hh * hd:(hh + 1) * hd].astype(BF16)
        k_ref[0, hh] = k[:, hh * hd:(hh + 1) * hd].astype(BF16)
        vt_ref[0, hh] = vt[hh * hd:(hh + 1) * hd, :].astype(BF16)


def _proj_call(layer, x2, w_in, rc, ra, rb, dw_w, dw_b, ln_g, ln_b, sc_w, bsz, seq):
    t, d = x2.shape
    gw = d // N_MIXERS
    tm = min(PROJ_ROWS, seq)
    ns = seq // tm
    nt = t // tm
    hd = gw // DA_HEADS
    cur = lambda i: jnp.minimum(i, nt - 1)
    prv = lambda i: jnp.maximum(i - 1, 0)
    tok = lambda: pl.BlockSpec((tm, gw), lambda i: (cur(i), 0))
    rope_spec = lambda: pl.BlockSpec((tm, gw), lambda i: (cur(i) % ns, 0))
    head = lambda shape, imap: pl.BlockSpec(shape, imap)
    out_shape = (
        [jax.ShapeDtypeStruct((t, gw), F32), jax.ShapeDtypeStruct((bsz, gw, seq), BF16)]
        + [jax.ShapeDtypeStruct((bsz, DA_HEADS, seq, hd), BF16)] * 2
        + [jax.ShapeDtypeStruct((bsz, DA_HEADS, hd, seq), BF16)]
        + [jax.ShapeDtypeStruct((t, gw), BF16)] * 2
    )
    return pl.pallas_call(
        functools.partial(_proj_kernel, gw, ns),
        grid=(nt + 1,),
        in_specs=[pl.BlockSpec((tm, d), lambda i: (cur(i), 0)), _layer_spec(w_in, layer),
                  rope_spec(), rope_spec(), rope_spec(),
                  _layer_spec(dw_w, layer), _layer_spec(dw_b, layer), _layer_spec(ln_g, layer),
                  _layer_spec(ln_b, layer), _layer_spec(sc_w, layer)],
        out_specs=[tok(), pl.BlockSpec((1, gw, tm), lambda i: (cur(i) // ns, 0, cur(i) % ns)),
                   head((1, DA_HEADS, tm, hd), lambda i: (cur(i) // ns, 0, cur(i) % ns, 0)),
                   head((1, DA_HEADS, tm, hd), lambda i: (cur(i) // ns, 0, cur(i) % ns, 0)),
                   head((1, DA_HEADS, hd, tm), lambda i: (cur(i) // ns, 0, 0, cur(i) % ns)),
                   pl.BlockSpec((tm, gw), lambda i: (prv(i), 0)),
                   pl.BlockSpec((tm, gw), lambda i: (prv(i), 0))],
        out_shape=out_shape,
        scratch_shapes=[pltpu.VMEM((tm + 2 * CONV_HALO, gw), F32), pltpu.VMEM((tm + 2 * CONV_HALO, gw), F32),
                        pltpu.VMEM((tm, gw), F32)],
        compiler_params=_cparams(("arbitrary",)),
        name="proj_conv",
    )(x2, w_in, rc, ra, rb, dw_w, dw_b, ln_g, ln_b, sc_w)


def _s5_kernel(nc, ut_ref, kv_ref, pa_ref, pb_ref, bq1_ref, bq2_ref, qa_ref, qb_ref, c1_ref, c2_ref,
               p1_ref, p2_ref, y_ref, t_ref, bc_ref):
    bsz, pch, _, lc = ut_ref.shape
    rows = bsz * nc
    nst = pa_ref.shape[-1] // 4

    def build(q, carry):
        kvq = kv_ref[pl.ds(pl.multiple_of(q * pch, pch), pch), :]
        for p in range(pch):
            x = jnp.broadcast_to(kvq[p:p + 1, :], (lc, 2 * lc))
            tile = pltpu.roll(x, 0, axis=1, stride=1, stride_axis=0)[:, lc:]
            t_ref[pl.ds(pl.multiple_of(q * lc, lc), lc), p * lc:(p + 1) * lc] = tile.astype(t_ref.dtype)
        return carry

    lax.fori_loop(0, pch, build, 0)

    pa, pb = pa_ref[...], pb_ref[...]
    for q in range(pch):
        bc_ref[q * lc:(q + 1) * lc, :] = (pa * bq1_ref[q:q + 1, :] + pb * bq2_ref[q:q + 1, :]).astype(bc_ref.dtype)

    v = jnp.concatenate([ut_ref[:, q].reshape(rows, lc) for q in range(pch)], axis=1)
    y = jnp.dot(v, t_ref[...], preferred_element_type=F32)
    w = jnp.dot(v, bc_ref[...], preferred_element_type=F32)

    cidx = lax.broadcasted_iota(jnp.int32, (bsz, nc, 2 * nst), 1).reshape(rows, 2 * nst)

    def cmul(x, lvl, half):
        lo, hi = half * 2 * nst, (half + 1) * 2 * nst
        return x * p1_ref[lvl:lvl + 1, lo:hi] + pltpu.roll(x, nst, axis=1) * p2_ref[lvl:lvl + 1, lo:hi]

    def chunk_scan(x, half, reverse):
        lvl, k = 0, 1
        while k < nc:
            sh = pltpu.roll(x, (rows - k) if reverse else k, axis=0)
            keep = (cidx <= nc - 1 - k) if reverse else (cidx >= k)
            x = x + cmul(jnp.where(keep, sh, 0.0), lvl, half)
            lvl, k = lvl + 1, 2 * k
        sh = pltpu.roll(x, (rows - 1) if reverse else 1, axis=0)
        keep = (cidx <= nc - 2) if reverse else (cidx >= 1)
        return jnp.where(keep, sh, 0.0)

    xprev = jnp.concatenate([chunk_scan(w[:, 0:2 * nst], 0, False),
                             chunk_scan(w[:, 2 * nst:4 * nst], 1, True)], axis=1)
    qa, qb = qa_ref[...], qb_ref[...]
    cc = jnp.concatenate([(qa * c1_ref[:, p:p + 1] + qb * c2_ref[:, p:p + 1]).astype(BF16)
                          for p in range(pch)], axis=1)
    y = y + jnp.dot(xprev.astype(BF16), cc, preferred_element_type=F32)
    for p in range(pch):
        y_ref[:, p] = y[:, p * lc:(p + 1) * lc].reshape(bsz, nc, lc)


def _s5_call(layer, ut4, tables):
    bsz, gw, nc, lc = ut4.shape
    ng = tables[0].shape[1]
    pch = gw // ng
    gspec = lambda a: pl.BlockSpec((None, None) + a.shape[2:], lambda g: (layer, g) + (0,) * (a.ndim - 2))
    io = lambda: pl.BlockSpec((bsz, pch, nc, lc), lambda g: (0, g, 0, 0))
    nst4 = tables[1].shape[-1]
    return pl.pallas_call(
        functools.partial(_s5_kernel, nc),
        grid=(ng,),
        in_specs=[io()] + [gspec(a) for a in tables],
        out_specs=io(),
        out_shape=jax.ShapeDtypeStruct((bsz, gw, nc, lc), F32),
        scratch_shapes=[pltpu.VMEM((pch * lc, pch * lc), BF16), pltpu.VMEM((pch * lc, nst4), BF16)],
        compiler_params=_cparams(("parallel",)),
        name="s5_mix",
    )(ut4, *tables)


def _s5_params(a_re, a_im, log_step, b_re, b_im, c_re, c_im, nc):
    lc = S5_CHUNK
    hi = lax.Precision.HIGHEST
    step = jnp.exp(log_step.astype(F32))[..., None]
    lr, li = a_re.astype(F32), a_im.astype(F32)
    mag = jnp.exp(lr * step)
    abr, abi = mag * jnp.cos(li * step), mag * jnp.sin(li * step)
    den = lr * lr + li * li
    pr = abr - 1.0
    fr = (pr * lr + abi * li) / den
    fi = (abi * lr - pr * li) / den
    br, bi = b_re.astype(F32), b_im.astype(F32)
    bbr = fr[..., None] * br - fi[..., None] * bi
    bbi = fr[..., None] * bi + fi[..., None] * br
    cr, ci = c_re.astype(F32), c_im.astype(F32)
    ls, an = lr * step, li * step
    nly, _, ng, nst = ls.shape
    pch = br.shape[-1]

    def apow(d, ks):
        m = jnp.exp(ls[:, d, :, :, None] * ks)
        return m * jnp.cos(an[:, d, :, :, None] * ks), m * jnp.sin(an[:, d, :, :, None] * ks)

    def apow_t(d, ks):
        m = jnp.exp(ks[:, None] * ls[:, d, :, None, :])
        return m * jnp.cos(ks[:, None] * an[:, d, :, None, :]), m * jnp.sin(ks[:, None] * an[:, d, :, None, :])

    kk = jnp.arange(lc, dtype=F32)
    bt_r, bt_i = jnp.swapaxes(bbr, -1, -2), jnp.swapaxes(bbi, -1, -2)

    def c_times_b(d):
        cb_r = bt_r[:, d, :, :, None, :] * cr[:, d, :, None] - bt_i[:, d, :, :, None, :] * ci[:, d, :, None]
        cb_i = bt_r[:, d, :, :, None, :] * ci[:, d, :, None] + bt_i[:, d, :, :, None, :] * cr[:, d, :, None]
        return cb_r.reshape(nly, ng, pch * pch, nst), cb_i.reshape(nly, ng, pch * pch, nst)

    def lag_kernels(cb, pw):
        return (jnp.einsum('lgxn,lgnk->lgxk', cb[0], pw[0], precision=hi)
                - jnp.einsum('lgxn,lgnk->lgxk', cb[1], pw[1], precision=hi))

    cb_f, cb_b = c_times_b(0), c_times_b(1)
    qb_r, qb_i = apow(1, lc - kk)
    left = jnp.where(kk > 0, lag_kernels(cb_b, (qb_r, qb_i)), 0.0)
    right = lag_kernels(cb_f, apow(0, kk)) + jnp.where(kk == 0, jnp.sum(cb_b[0], axis=-1, keepdims=True), 0.0)
    kv = jnp.concatenate([left, right], axis=-1)

    f_r, f_i = apow_t(0, lc - 1 - kk)
    b_r, b_i = apow_t(1, kk)
    pa = jnp.concatenate([f_r, f_i, b_r, b_i], axis=-1)
    pb = jnp.concatenate([-f_i, f_r, -b_i, b_r], axis=-1)
    bq1 = jnp.concatenate([bt_r[:, 0], bt_r[:, 0], bt_r[:, 1], bt_r[:, 1]], axis=-1)
    bq2 = jnp.concatenate([bt_i[:, 0], bt_i[:, 0], bt_i[:, 1], bt_i[:, 1]], axis=-1)

    qf_r, qf_i = apow(0, kk + 1.0)
    qa = jnp.concatenate([qf_r, -qf_i, qb_r, -qb_i], axis=2)
    qb = jnp.concatenate([-qf_i, -qf_r, -qb_i, -qb_r], axis=2)
    ct_r, ct_i = jnp.swapaxes(cr, -1, -2), jnp.swapaxes(ci, -1, -2)
    c1 = jnp.concatenate([ct_r[:, 0], ct_r[:, 0], ct_r[:, 1], ct_r[:, 1]], axis=2)
    c2 = jnp.concatenate([ct_i[:, 0], ct_i[:, 0], ct_i[:, 1], ct_i[:, 1]], axis=2)

    nlev = max(1, (nc - 1).bit_length())
    lev = float(lc) * 2.0 ** jnp.arange(nlev, dtype=F32)
    (lf_r, lf_i), (lb_r, lb_i) = apow_t(0, lev), apow_t(1, lev)
    p1 = jnp.concatenate([lf_r, lf_r, lb_r, lb_r], axis=-1)
    p2 = jnp.concatenate([-lf_i, lf_i, -lb_i, lb_i], axis=-1)
    return kv, pa, pb, bq1, bq2, qa, qb, c1, c2, p1, p2


def _attn_kernel(tk, lam_ref, q_ref, qn_ref, k_ref, vt_ref, g_ref, o_ref, va_ref, s_ref, mb_ref):
    hd = q_ref.shape[-1]
    dq = hd // 2
    tq = q_ref.shape[2]
    seq = k_ref.shape[2]
    rows_aug = va_ref.shape[0]

    def split_maps(q):
        lane = lax.broadcasted_iota(jnp.int32, q.shape, 1)
        zero = jnp.zeros_like(q)
        return jnp.concatenate([jnp.where(lane < dq, q, zero), jnp.where(lane >= dq, q, zero)], axis=0)

    nblk = seq // tk
    kc = min(tk, 2 * LANES)

    def scores(qq, i, slot):
        kb = k_ref[0, 0, pl.ds(pl.multiple_of(i * tk, tk), tk), :]
        s = lax.dot_general(kb, qq, (((1,), (1,)), ((), ())), preferred_element_type=F32)
        s_ref[slot] = s
        return jnp.max(s, axis=0, keepdims=True)

    qq = split_maps(q_ref[0, 0])

    @pl.when(pl.program_id(2) == 0)
    def _():
        va_ref[0:hd, :] = vt_ref[0, 0]
        row = lax.broadcasted_iota(jnp.int32, (rows_aug - hd, seq), 0)
        va_ref[hd:rows_aug, :] = jnp.where(row == 0, 1.0, 0.0).astype(va_ref.dtype)
        mb_ref[...] = scores(qq, 0, 0)

    def softmax_pv(i, slot, m, mb, acc1, acc2):
        m_new = jnp.maximum(m, mb)
        alpha = jnp.exp2(m - m_new)
        d1 = None
        d2 = None
        for c in range(tk // kc):
            p = jnp.exp2(s_ref[slot, c * kc:(c + 1) * kc, :] - m_new).astype(BF16)
            vb = va_ref[:, pl.ds(pl.multiple_of(i * tk + c * kc, kc), kc)]
            t1 = jnp.dot(vb, p[:, 0:tq], preferred_element_type=F32)
            t2 = jnp.dot(vb, p[:, tq:2 * tq], preferred_element_type=F32)
            d1 = t1 if d1 is None else d1 + t1
            d2 = t2 if d2 is None else d2 + t2
        return m_new, acc1 * alpha[:, 0:tq] + d1, acc2 * alpha[:, tq:2 * tq] + d2

    m = jnp.full((1, 2 * tq), -jnp.inf, F32)
    acc1 = jnp.zeros((rows_aug, tq), F32)
    acc2 = acc1
    assert nblk % 2 == 0
    mb = mb_ref[...]

    def pair(j, carry):
        m, mb, acc1, acc2 = carry
        mb1 = scores(qq, 2 * j + 1, 1)
        m, acc1, acc2 = softmax_pv(2 * j, 0, m, mb, acc1, acc2)
        mb2 = scores(qq, 2 * j + 2, 0)
        m, acc1, acc2 = softmax_pv(2 * j + 1, 1, m, mb1, acc1, acc2)
        return m, mb2, acc1, acc2

    m, mb, acc1, acc2 = lax.fori_loop(0, nblk // 2 - 1, pair, (m, mb, acc1, acc2))
    mb1 = scores(qq, nblk - 1, 1)
    m, acc1, acc2 = softmax_pv(nblk - 2, 0, m, mb, acc1, acc2)
    mb_ref[...] = scores(split_maps(qn_ref[0, 0]), 0, 0)
    m, acc1, acc2 = softmax_pv(nblk - 1, 1, m, mb1, acc1, acc2)
    lam = lam_ref[0]
    post = lam_ref[1]
    o = acc1[0:hd, :] / acc1[hd:hd + 1, :] - lam * (acc2[0:hd, :] / acc2[hd:hd + 1, :])
    o = o * lax.rsqrt(jnp.mean(o * o, axis=0, keepdims=True) + LN_EPS)
    o_ref[0] = (o * g_ref[...] * post).astype(o_ref.dtype)


def _attn_call(lam2, q, k, vt, g_col):
    bsz, nh, seq, hd = q.shape
    tq = min(ATT_Q, seq)
    tk = min(ATT_K, seq)
    rows_aug = hd + 16
    nq = seq // tq
    return pl.pallas_call(
        functools.partial(_attn_kernel, tk),
        grid=(bsz, nh, nq),
        in_specs=[pl.BlockSpec(memory_space=pltpu.SMEM),
                  pl.BlockSpec((1, 1, tq, hd), lambda b, h, i: (b, h, i, 0)),
                  pl.BlockSpec((1, 1, tq, hd), lambda b, h, i: (b, h, jnp.minimum(i + 1, nq - 1), 0)),
                  pl.BlockSpec((1, 1, seq, hd), lambda b, h, i: (b, h, 0, 0)),
                  pl.BlockSpec((1, 1, hd, seq), lambda b, h, i: (b, h, 0, 0)),
                  _const_spec(g_col.shape)],
        out_specs=pl.BlockSpec((1, hd, tq), lambda b, h, i: (b, h, i)),
        out_shape=jax.ShapeDtypeStruct((bsz, nh * hd, seq), BF16),
        scratch_shapes=[pltpu.VMEM((rows_aug, seq), BF16), pltpu.VMEM((2, tk, 2 * tq), F32),
                        pltpu.VMEM((1, 2 * tq), F32)],
        compiler_params=_cparams(("parallel", "parallel", "arbitrary")),
        name="diff_attn",
    )(lam2, q, q, k, vt, g_col)


def _gelu_tanh(x):
    return 0.5 * x * (1.0 + jnp.tanh(math.sqrt(2.0 / math.pi) * (x + 0.044715 * (x * x * x))))


def _ffn_kernel(alpha, splits, x_ref, ya_ref, u_ref, yst_ref, yc_ref, ydt_ref,
                wo_ref, sd_ref, wg_ref, bg_ref, g1_ref, b1_ref, w1_ref, w3_ref, w2_ref,
                g2_ref, b2_ref, o_ref):
    gw = ya_ref.shape[-1]
    x = x_ref[...]
    ys = _gelu_tanh(sd_ref[...] * u_ref[...] + yst_ref[0].T)
    gate = jnp.dot(ys.astype(BF16), wg_ref[...], preferred_element_type=F32) + bg_ref[...]
    y_b = (ys * _sigmoid(gate)).astype(BF16)
    mix = jnp.dot(ya_ref[...], wo_ref[0:gw, :], preferred_element_type=F32)
    mix += jnp.dot(y_b, wo_ref[gw:2 * gw, :], preferred_element_type=F32)
    mix += jnp.dot(yc_ref[...], wo_ref[2 * gw:3 * gw, :], preferred_element_type=F32)
    mix += lax.dot_general(ydt_ref[0], wo_ref[3 * gw:4 * gw, :], (((0,), (0,)), ((), ())),
                           preferred_element_type=F32)
    x1 = _layer_norm(alpha * x + mix, g1_ref[...], b1_ref[...])
    x1b = x1.astype(BF16)
    ff = None
    for lo, hi in splits:
        h1 = jnp.dot(x1b, w1_ref[:, lo:hi], preferred_element_type=F32)
        h3 = jnp.dot(x1b, w3_ref[:, lo:hi], preferred_element_type=F32)
        act = (h1 * _sigmoid(h1) * h3).astype(BF16)
        part = jnp.dot(act, w2_ref[lo:hi, :], preferred_element_type=F32)
        ff = part if ff is None else ff + part
    o_ref[...] = _layer_norm(alpha * x1 + ff, g2_ref[...], b2_ref[...])


def _ffn_call(layer, alpha, x2, ya, u, yst, yc, ydt, wo, sd, wg, bgl, g1, b1, w1, w3, w2, g2, b2, bsz, seq):
    t, d = x2.shape
    gw = d // N_MIXERS
    tm = min(FFN_ROWS, seq)
    ns = seq // tm
    hid = w1.shape[-1]
    cut = (hid // 2 + 255) // 256 * 256
    splits = ((0, cut), (cut, hid)) if 0 < cut < hid else ((0, hid),)
    tok = lambda w: pl.BlockSpec((tm, w), lambda i: (i, 0))
    single = lambda a: _layer_spec(a, layer, pipeline_mode=pl.Buffered(1))
    return pl.pallas_call(
        functools.partial(_ffn_kernel, alpha, splits),
        grid=(t // tm,),
        in_specs=[tok(d), tok(gw), tok(gw),
                  pl.BlockSpec((1, gw, tm), lambda i: (i // ns, 0, i % ns)),
                  tok(gw),
                  pl.BlockSpec((1, gw, tm), lambda i: (i // ns, 0, i % ns)),
                  single(wo), single(sd), single(wg), single(bgl), single(g1), single(b1),
                  single(w1), single(w3), single(w2), single(g2), single(b2)],
        out_specs=tok(d),
        out_shape=jax.ShapeDtypeStruct((t, d), F32),
        compiler_params=_cparams(("parallel",)),
        name="out_ffn",
    )(x2, ya, u, yst, yc, ydt, wo, sd, wg, bgl, g1, b1, w1, w3, w2, g2, b2)


def _rope_tables(seq, gw):
    hd = gw // DA_HEADS
    dq = hd // 2
    rot = dq // 4
    half = rot // 2
    pos = jnp.arange(seq, dtype=F32)
    inv_freq = ROPE_THETA ** (-jnp.arange(0, rot, 2, dtype=F32) / rot)
    ang = pos[:, None] * inv_freq[None, :]
    cos, sin = jnp.cos(ang), jnp.sin(ang)
    lane = jnp.arange(gw) % dq
    idx = lane % half
    lo = (lane < half)[None, :]
    hi = ((lane >= half) & (lane < rot))[None, :]
    cos_l, sin_l = cos[:, idx], sin[:, idx]
    rc = jnp.where(lo | hi, cos_l, 1.0)
    ra = jnp.where(lo, -sin_l, 0.0)
    rb = jnp.where(hi, sin_l, 0.0)
    return rc.astype(F32), ra.astype(F32), rb.astype(F32)


def kernel(x, w_in, w_out, conf_dw_w, conf_dw_b, conf_ln_g, conf_ln_b, s5_a_re, s5_a_im, s5_log_step,
           s5_b_re, s5_b_im, s5_c_re, s5_c_im, s5_d, s5_w_glu, s5_b_glu, sc_conv_w, da_lq1, da_lk1,
           da_lq2, da_lk2, da_subln_g, ln1_g, ln1_b, w_ffn1, w_ffn3, w_ffn2, ln2_g, ln2_b):
    bsz, seq, d = x.shape
    depth = w_in.shape[0]
    gw = d // N_MIXERS
    alpha = (2.0 * depth) ** 0.25
    rc, ra, rb = _rope_tables(seq, gw)
    rows = lambda a: a.astype(F32).reshape(depth, 1, -1)
    bf = lambda a: a.astype(BF16)
    w_in, w_out, s5_w_glu, w_ffn1, w_ffn3, w_ffn2 = map(bf, (w_in, w_out, s5_w_glu, w_ffn1, w_ffn3, w_ffn2))
    s5_d, s5_b_glu, ln1_g, ln1_b, ln2_g, ln2_b = map(rows, (s5_d, s5_b_glu, ln1_g, ln1_b, ln2_g, ln2_b))
    conf_dw_b, conf_ln_g, conf_ln_b = map(rows, (conf_dw_b, conf_ln_g, conf_ln_b))
    conf_dw_w, sc_conv_w = conf_dw_w.astype(F32), sc_conv_w.astype(F32)
    nc = seq // S5_CHUNK
    s5p = _s5_params(s5_a_re, s5_a_im, s5_log_step, s5_b_re, s5_b_im, s5_c_re, s5_c_im, nc)
    x2 = x.reshape(bsz * seq, d)
    for l in range(depth):
        u, ut, q, k, vt, ya, yc = _proj_call(l, x2, w_in, rc, ra, rb, conf_dw_w, conf_dw_b, conf_ln_g,
                                             conf_ln_b, sc_conv_w, bsz, seq)
        yst = _s5_call(l, ut.reshape(bsz, gw, nc, S5_CHUNK), s5p).reshape(bsz, gw, seq)
        lam_init = 0.8 - 0.6 * math.exp(-0.3 * l)
        lam = (jnp.exp(jnp.sum(da_lq1[l].astype(F32) * da_lk1[l].astype(F32)))
               - jnp.exp(jnp.sum(da_lq2[l].astype(F32) * da_lk2[l].astype(F32))) + lam_init)
        lam2 = jnp.stack([lam, jnp.asarray(1.0 - lam_init, F32)]).astype(F32)
        ydt = _attn_call(lam2, q, k, vt, da_subln_g[l].astype(F32).reshape(-1, 1))
        x2 = _ffn_call(l, alpha, x2, ya, u, yst, yc, ydt,
                       w_out, s5_d, s5_w_glu, s5_b_glu, ln1_g, ln1_b, w_ffn1, w_ffn3, w_ffn2, ln2_g, ln2_b,
                       bsz, seq)
    return x2.reshape(bsz, seq, d)
```

```python
import functools
import math

import jax
import jax.numpy as jnp
from jax import lax
from jax.experimental import pallas as pl
from jax.experimental.pallas import tpu as pltpu

F32 = jnp.float32
BF16 = jnp.bfloat16

N_MIXERS = 4
CONV_K = 31
S5_CH = 16
S5_STATE = 64
SHORT_K = 3
DA_HEADS = 4
ROPE_THETA = 500000.0
LN_EPS = 1e-5

SUBLANES = 8
LANES = 128
VMEM_LIMIT_BYTES = 56 * 1024 * 1024

PROJ_ROWS = 512
CONV_HALO = 16
CONV_CHUNK = 128
S5_CHUNK = LANES
ATT_Q = 256
ATT_K = 1024
FFN_ROWS = 512


def _cparams(sem):
    return pltpu.CompilerParams(dimension_semantics=sem, vmem_limit_bytes=VMEM_LIMIT_BYTES)


def _const_spec(shape):
    nd = len(shape)
    return pl.BlockSpec(shape, lambda *_: (0,) * nd)


def _layer_spec(a, layer, **kw):
    return pl.BlockSpec((None,) + a.shape[1:], lambda *_: (layer,) + (0,) * (a.ndim - 1), **kw)


def _sigmoid(x):
    return 1.0 / (1.0 + jnp.exp(-x))


def _layer_norm(x, g, b):
    mu = jnp.mean(x, axis=-1, keepdims=True)
    xc = x - mu
    var = jnp.mean(xc * xc, axis=-1, keepdims=True)
    return xc * lax.rsqrt(var + LN_EPS) * g + b


def _conv_taps(buf, base, rows, w_ref, ntaps):
    win_rows = rows + 2 * CONV_HALO
    win = buf[base:base + win_rows, :]
    first = CONV_HALO - ntaps // 2
    acc = jnp.zeros((rows, win.shape[-1]), F32)
    for phase in range(SUBLANES):
        offs = [o for o in range(first, first + ntaps) if o % SUBLANES == phase]
        if not offs:
            continue
        rolled = win if phase == 0 else pltpu.roll(win, win_rows - phase, axis=0)
        for o in offs:
            lo = o - phase
            acc = acc + w_ref[o - first:o - first + 1, :] * rolled[lo:lo + rows, :]
    return acc


def _proj_kernel(gw, tiles_per_seq, x_ref, w_ref, rc_ref, ra_ref, rb_ref, dw_ref, db_ref, g_ref, b_ref, sw_ref,
                 u_ref, ut_ref, q_ref, k_ref, vt_ref, ya_ref, yc_ref, zwin, cwin, bghold):
    i = pl.program_id(0)
    tm = x_ref.shape[0]
    h = CONV_HALO
    xb = x_ref[...].astype(BF16)

    def seg(j):
        return jnp.dot(xb, w_ref[:, j * gw:(j + 1) * gw], preferred_element_type=F32)

    @pl.when(i == 0)
    def _():
        zwin[...] = jnp.zeros_like(zwin)
        cwin[...] = jnp.zeros_like(cwin)
        bghold[...] = jnp.zeros_like(bghold)

    za = seg(0) * _sigmoid(seg(1))
    bg = seg(3)
    cv = seg(4) * seg(5)

    keep_next = jnp.where(i % tiles_per_seq != 0, 1.0, 0.0).astype(F32)
    zwin[h + tm:2 * h + tm, :] = za[0:h, :] * keep_next
    cwin[h + tm:2 * h + tm, :] = cv[0:h, :] * keep_next
    rows = CONV_CHUNK
    for c in range(tm // rows):
        base = c * rows
        z = _layer_norm(_conv_taps(zwin, base, rows, dw_ref, CONV_K) + db_ref[...], g_ref[...], b_ref[...])
        ya_ref[base:base + rows, :] = (z * _sigmoid(z)).astype(ya_ref.dtype)
        sc = _conv_taps(cwin, base, rows, sw_ref, SHORT_K)
        yc_ref[base:base + rows, :] = (bghold[base:base + rows, :] * sc).astype(yc_ref.dtype)
    zwin[0:h, :] = zwin[tm:tm + h, :] * keep_next
    cwin[0:h, :] = cwin[tm:tm + h, :] * keep_next
    zwin[h:h + tm, :] = za
    cwin[h:h + tm, :] = cv
    bghold[...] = bg

    u = seg(2)
    u_ref[...] = u
    ut_ref[0] = u.T.astype(BF16)

    rc, ra, rb = rc_ref[...], ra_ref[...], rb_ref[...]
    hd = gw // DA_HEADS
    half = hd // 2 // 4 // 2

    def rope(t):
        return t * rc + pltpu.roll(t, gw - half, axis=1) * ra + pltpu.roll(t, half, axis=1) * rb

    scale = (hd // 2) ** -0.5 * math.log2(math.e)
    q = rope(seg(6)) * scale
    k = rope(seg(7))
    vt = seg(8).T
    for hh in range(DA_HEADS):
        q_ref[0, hh] = q[:, hh * hd:(hh + 1) * hd].astype(BF16)
        k_ref[0, hh] = k[:, hh * hd:(hh + 1) * hd].astype(BF16)
        vt_ref[0, hh] = vt[hh * hd:(hh + 1) * hd, :].astype(BF16)


def _proj_call(layer, x2, w_in, rc, ra, rb, dw_w, dw_b, ln_g, ln_b, sc_w, bsz, seq):
    t, d = x2.shape
    gw = d // N_MIXERS
    tm = min(PROJ_ROWS, seq)
    ns = seq // tm
    nt = t // tm
    hd = gw // DA_HEADS
    cur = lambda i: jnp.minimum(i, nt - 1)
    prv = lambda i: jnp.maximum(i - 1, 0)
    tok = lambda: pl.BlockSpec((tm, gw), lambda i: (cur(i), 0))
    rope_spec = lambda: pl.BlockSpec((tm, gw), lambda i: (cur(i) % ns, 0))
    head = lambda shape, imap: pl.BlockSpec(shape, imap)
    out_shape = (
        [jax.ShapeDtypeStruct((t, gw), F32), jax.ShapeDtypeStruct((bsz, gw, seq), BF16)]
        + [jax.ShapeDtypeStruct((bsz, DA_HEADS, seq, hd), BF16)] * 2
        + [jax.ShapeDtypeStruct((bsz, DA_HEADS, hd, seq), BF16)]
        + [jax.ShapeDtypeStruct((t, gw), BF16)] * 2
    )
    return pl.pallas_call(
        functools.partial(_proj_kernel, gw, ns),
        grid=(nt + 1,),
        in_specs=[pl.BlockSpec((tm, d), lambda i: (cur(i), 0)), _layer_spec(w_in, layer),
                  rope_spec(), rope_spec(), rope_spec(),
                  _layer_spec(dw_w, layer), _layer_spec(dw_b, layer), _layer_spec(ln_g, layer),
                  _layer_spec(ln_b, layer), _layer_spec(sc_w, layer)],
        out_specs=[tok(), pl.BlockSpec((1, gw, tm), lambda i: (cur(i) // ns, 0, cur(i) % ns)),
                   head((1, DA_HEADS, tm, hd), lambda i: (cur(i) // ns, 0, cur(i) % ns, 0)),
                   head((1, DA_HEADS, tm, hd), lambda i: (cur(i) // ns, 0, cur(i) % ns, 0)),
                   head((1, DA_HEADS, hd, tm), lambda i: (cur(i) // ns, 0, 0, cur(i) % ns)),
                   pl.BlockSpec((tm, gw), lambda i: (prv(i), 0)),
                   pl.BlockSpec((tm, gw), lambda i: (prv(i), 0))],
        out_shape=out_shape,
        scratch_shapes=[pltpu.VMEM((tm + 2 * CONV_HALO, gw), F32), pltpu.VMEM((tm + 2 * CONV_HALO, gw), F32),
                        pltpu.VMEM((tm, gw), F32)],
        compiler_params=_cparams(("arbitrary",)),
        name="proj_conv",
    )(x2, w_in, rc, ra, rb, dw_w, dw_b, ln_g, ln_b, sc_w)


def _s5_kernel(nc, ut_ref, kv_ref, pa_ref, pb_ref, bq1_ref, bq2_ref, qa_ref, qb_ref, c1_ref, c2_ref,
               p1_ref, p2_ref, y_ref, t_ref, bc_ref):
    bsz, pch, _, lc = ut_ref.shape
    rows = bsz * nc
    nst = pa_ref.shape[-1] // 4

    def build(q, carry):
        kvq = kv_ref[pl.ds(pl.multiple_of(q * pch, pch), pch), :]
        for p in range(pch):
            x = jnp.broadcast_to(kvq[p:p + 1, :], (lc, 2 * lc))
            tile = pltpu.roll(x, 0, axis=1, stride=1, stride_axis=0)[:, lc:]
            t_ref[pl.ds(pl.multiple_of(q * lc, lc), lc), p * lc:(p + 1) * lc] = tile.astype(t_ref.dtype)
        return carry

    lax.fori_loop(0, pch, build, 0)

    pa, pb = pa_ref[...], pb_ref[...]
    for q in range(pch):
        bc_ref[q * lc:(q + 1) * lc, :] = (pa * bq1_ref[q:q + 1, :] + pb * bq2_ref[q:q + 1, :]).astype(bc_ref.dtype)

    v = jnp.concatenate([ut_ref[:, q].reshape(rows, lc) for q in range(pch)], axis=1)
    y = jnp.dot(v, t_ref[...], preferred_element_type=F32)
    w = jnp.dot(v, bc_ref[...], preferred_element_type=F32)

    cidx = lax.broadcasted_iota(jnp.int32, (bsz, nc, 2 * nst), 1).reshape(rows, 2 * nst)

    def cmul(x, lvl, half):
        lo, hi = half * 2 * nst, (half + 1) * 2 * nst
        return x * p1_ref[lvl:lvl + 1, lo:hi] + pltpu.roll(x, nst, axis=1) * p2_ref[lvl:lvl + 1, lo:hi]

    def chunk_scan(x, half, reverse):
        lvl, k = 0, 1
        while k < nc:
            sh = pltpu.roll(x, (rows - k) if reverse else k, axis=0)
            keep = (cidx <= nc - 1 - k) if reverse else (cidx >= k)
            x = x + cmul(jnp.where(keep, sh, 0.0), lvl, half)
            lvl, k = lvl + 1, 2 * k
        sh = pltpu.roll(x, (rows - 1) if reverse else 1, axis=0)
        keep = (cidx <= nc - 2) if reverse else (cidx >= 1)
        return jnp.where(keep, sh, 0.0)

    xprev = jnp.concatenate([chunk_scan(w[:, 0:2 * nst], 0, False),
                             chunk_scan(w[:, 2 * nst:4 * nst], 1, True)], axis=1)
    qa, qb = qa_ref[...], qb_ref[...]
    cc = jnp.concatenate([(qa * c1_ref[:, p:p + 1] + qb * c2_ref[:, p:p + 1]).astype(BF16)
                          for p in range(pch)], axis=1)
    y = y + jnp.dot(xprev.astype(BF16), cc, preferred_element_type=F32)
    for p in range(pch):
        y_ref[:, p] = y[:, p * lc:(p + 1) * lc].reshape(bsz, nc, lc)


def _s5_call(layer, ut4, tables):
    bsz, gw, nc, lc = ut4.shape
    ng = tables[0].shape[1]
    pch = gw // ng
    gspec = lambda a: pl.BlockSpec((None, None) + a.shape[2:], lambda g: (layer, g) + (0,) * (a.ndim - 2))
    io = lambda: pl.BlockSpec((bsz, pch, nc, lc), lambda g: (0, g, 0, 0))
    nst4 = tables[1].shape[-1]
    return pl.pallas_call(
        functools.partial(_s5_kernel, nc),
        grid=(ng,),
        in_specs=[io()] + [gspec(a) for a in tables],
        out_specs=io(),
        out_shape=jax.ShapeDtypeStruct((bsz, gw, nc, lc), F32),
        scratch_shapes=[pltpu.VMEM((pch * lc, pch * lc), BF16), pltpu.VMEM((pch * lc, nst4), BF16)],
        compiler_params=_cparams(("parallel",)),
        name="s5_mix",
    )(ut4, *tables)


def _s5_params(a_re, a_im, log_step, b_re, b_im, c_re, c_im, nc):
    lc = S5_CHUNK
    hi = lax.Precision.HIGHEST
    step = jnp.exp(log_step.astype(F32))[..., None]
    lr, li = a_re.astype(F32), a_im.astype(F32)
    mag = jnp.exp(lr * step)
    abr, abi = mag * jnp.cos(li * step), mag * jnp.sin(li * step)
    den = lr * lr + li * li
    pr = abr - 1.0
    fr = (pr * lr + abi * li) / den
    fi = (abi * lr - pr * li) / den
    br, bi = b_re.astype(F32), b_im.astype(F32)
    bbr = fr[..., None] * br - fi[..., None] * bi
    bbi = fr[..., None] * bi + fi[..., None] * br
    cr, ci = c_re.astype(F32), c_im.astype(F32)
    ls, an = lr * step, li * step
    nly, _, ng, nst = ls.shape
    pch = br.shape[-1]

    def apow(d, ks):
        m = jnp.exp(ls[:, d, :, :, None] * ks)
        return m * jnp.cos(an[:, d, :, :, None] * ks), m * jnp.sin(an[:, d, :, :, None] * ks)

    def apow_t(d, ks):
        m = jnp.exp(ks[:, None] * ls[:, d, :, None, :])
        return m * jnp.cos(ks[:, None] * an[:, d, :, None, :]), m * jnp.sin(ks[:, None] * an[:, d, :, None, :])

    kk = jnp.arange(lc, dtype=F32)
    bt_r, bt_i = jnp.swapaxes(bbr, -1, -2), jnp.swapaxes(bbi, -1, -2)

    def c_times_b(d):
        cb_r = bt_r[:, d, :, :, None, :] * cr[:, d, :, None] - bt_i[:, d, :, :, None, :] * ci[:, d, :, None]
        cb_i = bt_r[:, d, :, :, None, :] * ci[:, d, :, None] + bt_i[:, d, :, :, None, :] * cr[:, d, :, None]
        return cb_r.reshape(nly, ng, pch * pch, nst), cb_i.reshape(nly, ng, pch * pch, nst)

    def lag_kernels(cb, pw):
        return (jnp.einsum('lgxn,lgnk->lgxk', cb[0], pw[0], precision=hi)
                - jnp.einsum('lgxn,lgnk->lgxk', cb[1], pw[1], precision=hi))

    cb_f, cb_b = c_times_b(0), c_times_b(1)
    qb_r, qb_i = apow(1, lc - kk)
    left = jnp.where(kk > 0, lag_kernels(cb_b, (qb_r, qb_i)), 0.0)
    right = lag_kernels(cb_f, apow(0, kk)) + jnp.where(kk == 0, jnp.sum(cb_b[0], axis=-1, keepdims=True), 0.0)
    kv = jnp.concatenate([left, right], axis=-1)

    f_r, f_i = apow_t(0, lc - 1 - kk)
    b_r, b_i = apow_t(1, kk)
    pa = jnp.concatenate([f_r, f_i, b_r, b_i], axis=-1)
    pb = jnp.concatenate([-f_i, f_r, -b_i, b_r], axis=-1)
    bq1 = jnp.concatenate([bt_r[:, 0], bt_r[:, 0], bt_r[:, 1], bt_r[:, 1]], axis=-1)
    bq2 = jnp.concatenate([bt_i[:, 0], bt_i[:, 0], bt_i[:, 1], bt_i[:, 1]], axis=-1)

    qf_r, qf_i = apow(0, kk + 1.0)
    qa = jnp.concatenate([qf_r, -qf_i, qb_r, -qb_i], axis=2)
    qb = jnp.concatenate([-qf_i, -qf_r, -qb_i, -qb_r], axis=2)
    ct_r, ct_i = jnp.swapaxes(cr, -1, -2), jnp.swapaxes(ci, -1, -2)
    c1 = jnp.concatenate([ct_r[:, 0], ct_r[:, 0], ct_r[:, 1], ct_r[:, 1]], axis=2)
    c2 = jnp.concatenate([ct_i[:, 0], ct_i[:, 0], ct_i[:, 1], ct_i[:, 1]], axis=2)

    nlev = max(1, (nc - 1).bit_length())
    lev = float(lc) * 2.0 ** jnp.arange(nlev, dtype=F32)
    (lf_r, lf_i), (lb_r, lb_i) = apow_t(0, lev), apow_t(1, lev)
    p1 = jnp.concatenate([lf_r, lf_r, lb_r, lb_r], axis=-1)
    p2 = jnp.concatenate([-lf_i, lf_i, -lb_i, lb_i], axis=-1)
    return kv, pa, pb, bq1, bq2, qa, qb, c1, c2, p1, p2


def _attn_kernel(tk, lam_ref, q_ref, qn_ref, k_ref, vt_ref, g_ref, o_ref, va_ref, s_ref, mb_ref):
    hd = q_ref.shape[-1]
    dq = hd // 2
    tq = q_ref.shape[2]
    seq = k_ref.shape[2]
    rows_aug = va_ref.shape[0]

    def split_maps(q):
        lane = lax.broadcasted_iota(jnp.int32, q.shape, 1)
        zero = jnp.zeros_like(q)
        return jnp.concatenate([jnp.where(lane < dq, q, zero), jnp.where(lane >= dq, q, zero)], axis=0)

    nblk = seq // tk
    kc = min(tk, 2 * LANES)

    def scores(qq, i, slot):
        kb = k_ref[0, 0, pl.ds(pl.multiple_of(i * tk, tk), tk), :]
        s = lax.dot_general(kb, qq, (((1,), (1,)), ((), ())), preferred_element_type=F32)
        s_ref[slot] = s
        return jnp.max(s, axis=0, keepdims=True)

    qq = split_maps(q_ref[0, 0])

    @pl.when(pl.program_id(2) == 0)
    def _():
        va_ref[0:hd, :] = vt_ref[0, 0]
        row = lax.broadcasted_iota(jnp.int32, (rows_aug - hd, seq), 0)
        va_ref[hd:rows_aug, :] = jnp.where(row == 0, 1.0, 0.0).astype(va_ref.dtype)
        mb_ref[...] = scores(qq, 0, 0)

    def softmax_pv(i, slot, m, mb, acc1, acc2):
        m_new = jnp.maximum(m, mb)
        alpha = jnp.exp2(m - m_new)
        d1 = None
        d2 = None
        for c in range(tk // kc):
            p = jnp.exp2(s_ref[slot, c * kc:(c + 1) * kc, :] - m_new).astype(BF16)
            vb = va_ref[:, pl.ds(pl.multiple_of(i * tk + c * kc, kc), kc)]
            t1 = jnp.dot(vb, p[:, 0:tq], preferred_element_type=F32)
            t2 = jnp.dot(vb, p[:, tq:2 * tq], preferred_element_type=F32)
            d1 = t1 if d1 is None else d1 + t1
            d2 = t2 if d2 is None else d2 + t2
        return m_new, acc1 * alpha[:, 0:tq] + d1, acc2 * alpha[:, tq:2 * tq] + d2

    m = jnp.full((1, 2 * tq), -jnp.inf, F32)
    acc1 = jnp.zeros((rows_aug, tq), F32)
    acc2 = acc1
    assert nblk % 2 == 0
    mb = mb_ref[...]

    def pair(j, carry):
        m, mb, acc1, acc2 = carry
        mb1 = scores(qq, 2 * j + 1, 1)
        m, acc1, acc2 = softmax_pv(2 * j, 0, m, mb, acc1, acc2)
        mb2 = scores(qq, 2 * j + 2, 0)
        m, acc1, acc2 = softmax_pv(2 * j + 1, 1, m, mb1, acc1, acc2)
        return m, mb2, acc1, acc2

    carry = (m, mb, acc1, acc2)
    for j in range(nblk // 2 - 1):
        carry = pair(j, carry)
    m, mb, acc1, acc2 = carry
    mb1 = scores(qq, nblk - 1, 1)
    m, acc1, acc2 = softmax_pv(nblk - 2, 0, m, mb, acc1, acc2)
    mb_ref[...] = scores(split_maps(qn_ref[0, 0]), 0, 0)
    m, acc1, acc2 = softmax_pv(nblk - 1, 1, m, mb1, acc1, acc2)
    lam = lam_ref[0]
    post = lam_ref[1]
    o = acc1[0:hd, :] / acc1[hd:hd + 1, :] - lam * (acc2[0:hd, :] / acc2[hd:hd + 1, :])
    o = o * lax.rsqrt(jnp.mean(o * o, axis=0, keepdims=True) + LN_EPS)
    o_ref[0] = (o * g_ref[...] * post).astype(o_ref.dtype)


def _attn_call(lam2, q, k, vt, g_col):
    bsz, nh, seq, hd = q.shape
    tq = min(ATT_Q, seq)
    tk = min(ATT_K, seq)
    rows_aug = hd + 16
    nq = seq // tq
    return pl.pallas_call(
        functools.partial(_attn_kernel, tk),
        grid=(bsz, nh, nq),
        in_specs=[pl.BlockSpec(memory_space=pltpu.SMEM),
                  pl.BlockSpec((1, 1, tq, hd), lambda b, h, i: (b, h, i, 0)),
                  pl.BlockSpec((1, 1, tq, hd), lambda b, h, i: (b, h, jnp.minimum(i + 1, nq - 1), 0)),
                  pl.BlockSpec((1, 1, seq, hd), lambda b, h, i: (b, h, 0, 0)),
                  pl.BlockSpec((1, 1, hd, seq), lambda b, h, i: (b, h, 0, 0)),
                  _const_spec(g_col.shape)],
        out_specs=pl.BlockSpec((1, hd, tq), lambda b, h, i: (b, h, i)),
        out_shape=jax.ShapeDtypeStruct((bsz, nh * hd, seq), BF16),
        scratch_shapes=[pltpu.VMEM((rows_aug, seq), BF16), pltpu.VMEM((2, tk, 2 * tq), F32),
                        pltpu.VMEM((1, 2 * tq), F32)],
        compiler_params=_cparams(("parallel", "parallel", "arbitrary")),
        name="diff_attn",
    )(lam2, q, q, k, vt, g_col)


def _gelu_tanh(x):
    return 0.5 * x * (1.0 + jnp.tanh(math.sqrt(2.0 / math.pi) * (x + 0.044715 * (x * x * x))))


def _ffn_kernel(alpha, splits, x_ref, ya_ref, u_ref, yst_ref, yc_ref, ydt_ref,
                wo_ref, sd_ref, wg_ref, bg_ref, g1_ref, b1_ref, w1_ref, w3_ref, w2_ref,
                g2_ref, b2_ref, o_ref):
    gw = ya_ref.shape[-1]
    x = x_ref[...]
    ys = _gelu_tanh(sd_ref[...] * u_ref[...] + yst_ref[0].T)
    gate = jnp.dot(ys.astype(BF16), wg_ref[...], preferred_element_type=F32) + bg_ref[...]
    y_b = (ys * _sigmoid(gate)).astype(BF16)
    mix = jnp.dot(ya_ref[...], wo_ref[0:gw, :], preferred_element_type=F32)
    mix += jnp.dot(y_b, wo_ref[gw:2 * gw, :], preferred_element_type=F32)
    mix += jnp.dot(yc_ref[...], wo_ref[2 * gw:3 * gw, :], preferred_element_type=F32)
    mix += lax.dot_general(ydt_ref[0], wo_ref[3 * gw:4 * gw, :], (((0,), (0,)), ((), ())),
                           preferred_element_type=F32)
    x1 = _layer_norm(alpha * x + mix, g1_ref[...], b1_ref[...])
    x1b = x1.astype(BF16)
    ff = None
    for lo, hi in splits:
        h1 = jnp.dot(x1b, w1_ref[:, lo:hi], preferred_element_type=F32)
        h3 = jnp.dot(x1b, w3_ref[:, lo:hi], preferred_element_type=F32)
        act = (h1 * _sigmoid(h1) * h3).astype(BF16)
        part = jnp.dot(act, w2_ref[lo:hi, :], preferred_element_type=F32)
        ff = part if ff is None else ff + part
    o_ref[...] = _layer_norm(alpha * x1 + ff, g2_ref[...], b2_ref[...])


def _ffn_call(layer, alpha, x2, ya, u, yst, yc, ydt, wo, sd, wg, bgl, g1, b1, w1, w3, w2, g2, b2, bsz, seq):
    t, d = x2.shape
    gw = d // N_MIXERS
    tm = min(FFN_ROWS, seq)
    ns = seq // tm
    hid = w1.shape[-1]
    cut = (hid // 2 + 255) // 256 * 256
    splits = ((0, cut), (cut, hid)) if 0 < cut < hid else ((0, hid),)
    tok = lambda w: pl.BlockSpec((tm, w), lambda i: (i, 0))
    single = lambda a: _layer_spec(a, layer, pipeline_mode=pl.Buffered(1))
    return pl.pallas_call(
        functools.partial(_ffn_kernel, alpha, splits),
        grid=(t // tm,),
        in_specs=[tok(d), tok(gw), tok(gw),
                  pl.BlockSpec((1, gw, tm), lambda i: (i // ns, 0, i % ns)),
                  tok(gw),
                  pl.BlockSpec((1, gw, tm), lambda i: (i // ns, 0, i % ns)),
                  single(wo), single(sd), single(wg), single(bgl), single(g1), single(b1),
                  single(w1), single(w3), single(w2), single(g2), single(b2)],
        out_specs=tok(d),
        out_shape=jax.ShapeDtypeStruct((t, d), F32),
        compiler_params=_cparams(("parallel",)),
        name="out_ffn",
    )(x2, ya, u, yst, yc, ydt, wo, sd, wg, bgl, g1, b1, w1, w3, w2, g2, b2)


def _rope_tables(seq, gw):
    hd = gw // DA_HEADS
    dq = hd // 2
    rot = dq // 4
    half = rot // 2
    pos = jnp.arange(seq, dtype=F32)
    inv_freq = ROPE_THETA ** (-jnp.arange(0, rot, 2, dtype=F32) / rot)
    ang = pos[:, None] * inv_freq[None, :]
    cos, sin = jnp.cos(ang), jnp.sin(ang)
    lane = jnp.arange(gw) % dq
    idx = lane % half
    lo = (lane < half)[None, :]
    hi = ((lane >= half) & (lane < rot))[None, :]
    cos_l, sin_l = cos[:, idx], sin[:, idx]
    rc = jnp.where(lo | hi, cos_l, 1.0)
    ra = jnp.where(lo, -sin_l, 0.0)
    rb = jnp.where(hi, sin_l, 0.0)
    return rc.astype(F32), ra.astype(F32), rb.astype(F32)


def kernel(x, w_in, w_out, conf_dw_w, conf_dw_b, conf_ln_g, conf_ln_b, s5_a_re, s5_a_im, s5_log_step,
           s5_b_re, s5_b_im, s5_c_re, s5_c_im, s5_d, s5_w_glu, s5_b_glu, sc_conv_w, da_lq1, da_lk1,
           da_lq2, da_lk2, da_subln_g, ln1_g, ln1_b, w_ffn1, w_ffn3, w_ffn2, ln2_g, ln2_b):
    bsz, seq, d = x.shape
    depth = w_in.shape[0]
    gw = d // N_MIXERS
    alpha = (2.0 * depth) ** 0.25
    rc, ra, rb = _rope_tables(seq, gw)
    rows = lambda a: a.astype(F32).reshape(depth, 1, -1)
    bf = lambda a: a.astype(BF16)
    w_in, w_out, s5_w_glu, w_ffn1, w_ffn3, w_ffn2 = map(bf, (w_in, w_out, s5_w_glu, w_ffn1, w_ffn3, w_ffn2))
    s5_d, s5_b_glu, ln1_g, ln1_b, ln2_g, ln2_b = map(rows, (s5_d, s5_b_glu, ln1_g, ln1_b, ln2_g, ln2_b))
    conf_dw_b, conf_ln_g, conf_ln_b = map(rows, (conf_dw_b, conf_ln_g, conf_ln_b))
    conf_dw_w, sc_conv_w = conf_dw_w.astype(F32), sc_conv_w.astype(F32)
    nc = seq // S5_CHUNK
    s5p = _s5_params(s5_a_re, s5_a_im, s5_log_step, s5_b_re, s5_b_im, s5_c_re, s5_c_im, nc)
    x2 = x.reshape(bsz * seq, d)
    for l in range(depth):
        u, ut, q, k, vt, ya, yc = _proj_call(l, x2, w_in, rc, ra, rb, conf_dw_w, conf_dw_b, conf_ln_g,
                                             conf_ln_b, sc_conv_w, bsz, seq)
        yst = _s5_call(l, ut.reshape(bsz, gw, nc, S5_CHUNK), s5p).reshape(bsz, gw, seq)
        lam_init = 0.8 - 0.6 * math.exp(-0.3 * l)
        lam = (jnp.exp(jnp.sum(da_lq1[l].astype(F32) * da_lk1[l].astype(F32)))
               - jnp.exp(jnp.sum(da_lq2[l].astype(F32) * da_lk2[l].astype(F32))) + lam_init)
        lam2 = jnp.stack([lam, jnp.asarray(1.0 - lam_init, F32)]).astype(F32)
        ydt = _attn_call(lam2, q, k, vt, da_subln_g[l].astype(F32).reshape(-1, 1))
        x2 = _ffn_call(l, alpha, x2, ya, u, yst, yc, ydt,
                       w_out, s5_d, s5_w_glu, s5_b_glu, ln1_g, ln1_b, w_ffn1, w_ffn3, w_ffn2, ln2_g, ln2_b,
                       bsz, seq)
    return x2.reshape(bsz, seq, d)
```
